```python
import math
import jax, jax.numpy as jnp
from jax import lax
import numpy as np


D_MODEL = 1024
BATCH = 8
SEQ = 2048
DEPTH = 2

D_MIX = D_MODEL
D_CONV = D_MIX // 4
D_LRU = D_MIX // 4
D_ATTN = D_MIX // 2
CONV_KERNEL = 31
LRU_HEADS = 4
LRU_HEAD_DIM = D_LRU // LRU_HEADS
LRU_CONV = 4
LRU_C = 8.0
HEAD_DIM = 64
N_Q_HEADS = D_ATTN // HEAD_DIM
N_KV_HEADS = 2
GQA_REP = N_Q_HEADS // N_KV_HEADS
KV_DIM = N_KV_HEADS * HEAD_DIM
WINDOW = 128
BLOCK = 128
REL_BUCKETS = 32
REL_MAX_DIST = 128
D_FF = 2816
ALPHA = (2.0 * DEPTH) ** 0.25
BETA = (8.0 * DEPTH) ** -0.25
LN_EPS = 1e-5
D_IN = 2 * D_CONV + 2 * D_LRU + D_ATTN + 2 * KV_DIM

kernel_name = 'hymba_conv_rglru_swa_macaron_deepnorm'


def _layernorm(x, g, b):
    xf = x.astype(jnp.float32)
    mu = jnp.mean(xf, axis=-1, keepdims=True)
    var = jnp.mean(jnp.square(xf - mu), axis=-1, keepdims=True)
    return ((xf - mu) * lax.rsqrt(var + LN_EPS)).astype(x.dtype) * g + b


def _swiglu(x, wg, wu, wd):
    return (jax.nn.silu(x @ wg) * (x @ wu)) @ wd


def _causal_dw_conv(x, w):
    k = w.shape[0]
    return lax.conv_general_dilated(
        x, w[:, None, :].astype(x.dtype), window_strides=(1,), padding=[(k - 1, 0)],
        dimension_numbers=('NWC', 'WIO', 'NWC'), feature_group_count=x.shape[-1])


def _conv_module(u, dw_w, dw_b, g, b):
    a, gate = jnp.split(u, 2, axis=-1)
    y = a * jax.nn.sigmoid(gate)
    y = _causal_dw_conv(y, dw_w) + dw_b
    y = _layernorm(y, g, b)
    return jax.nn.silu(y)


def _lin_combine(left, right):
    a1, b1 = left
    a2, b2 = right
    return a1 * a2, a2 * b1 + b2


def _recurrent(u, conv_w, conv_b, wa, ba, wx, bx, lam):
    B, S, _ = u.shape
    xb, gb = jnp.split(u, 2, axis=-1)
    xb = _causal_dw_conv(xb, conv_w) + conv_b
    xh = xb.reshape(B, S, LRU_HEADS, LRU_HEAD_DIM)
    r = jax.nn.sigmoid(jnp.einsum('bshi,hij->bshj', xh, wa).reshape(B, S, D_LRU) + ba)
    i = jax.nn.sigmoid(jnp.einsum('bshi,hij->bshj', xh, wx).reshape(B, S, D_LRU) + bx)
    log_a = LRU_C * r.astype(jnp.float32) * jax.nn.log_sigmoid(lam.astype(jnp.float32))
    a = jnp.exp(log_a)
    mult = jnp.sqrt(-jnp.expm1(2.0 * log_a))
    bterm = mult * (i * xb).astype(jnp.float32)
    _, h = lax.associative_scan(_lin_combine, (a, bterm), axis=1)
    return h.astype(u.dtype) * jax.nn.gelu(gb)


def _rel_bucket(dist):
    max_exact = REL_BUCKETS // 2
    is_small = dist < max_exact
    large = max_exact + (jnp.log(jnp.maximum(dist, 1).astype(jnp.float32) / max_exact)
                         / math.log(REL_MAX_DIST / max_exact)
                         * (REL_BUCKETS - max_exact)).astype(jnp.int32)
    large = jnp.minimum(large, REL_BUCKETS - 1)
    return jnp.where(is_small, dist, large)


def _band_bias_and_mask(rel_bias, seq):
    nb = seq // BLOCK
    qi = jnp.arange(BLOCK)[:, None]
    kj = jnp.arange(2 * BLOCK)[None, :]
    dist = qi - kj + BLOCK
    bucket = _rel_bucket(jnp.maximum(dist, 0))
    bias = rel_bias[bucket].astype(jnp.float32)
    bias = jnp.transpose(bias, (2, 0, 1)).reshape(N_KV_HEADS, GQA_REP, BLOCK, 2 * BLOCK)
    blk = jnp.arange(nb)[:, None, None]
    valid = (dist >= 0) & (dist < WINDOW) & (blk * BLOCK + kj - BLOCK >= 0)
    return bias, valid


def _band(t, nb):
    B = t.shape[0]
    tp = jnp.pad(t, ((0, 0), (BLOCK, 0), (0, 0), (0, 0)))
    tp = tp.reshape(B, nb + 1, BLOCK, t.shape[2], t.shape[3])
    return jnp.concatenate([tp[:, :-1], tp[:, 1:]], axis=2)


def _swa(q, k, v, band_bias, valid, sinks):
    B, S, _ = q.shape
    nb = S // BLOCK
    qb = q.reshape(B, nb, BLOCK, N_KV_HEADS, GQA_REP, HEAD_DIM)
    kb = _band(k.reshape(B, S, N_KV_HEADS, HEAD_DIM), nb)
    vb = _band(v.reshape(B, S, N_KV_HEADS, HEAD_DIM), nb)
    s = jnp.einsum('bnqgrd,bnkgd->bngrqk', qb, kb).astype(jnp.float32) * (HEAD_DIM ** -0.5)
    s = s + band_bias[None, None]
    s = jnp.where(valid[None, :, None, None], s, -1e30)
    sink = jnp.broadcast_to(sinks.astype(jnp.float32).reshape(1, 1, N_KV_HEADS, GQA_REP, 1, 1),
                            s.shape[:-1] + (1,))
    p = jax.nn.softmax(jnp.concatenate([s, sink], axis=-1), axis=-1)[..., :-1]
    o = jnp.einsum('bngrqk,bnkgd->bnqgrd', p.astype(vb.dtype), vb)
    return o.reshape(B, S, D_ATTN)


def _mixer(x, w_in, conv_dw_w, conv_dw_b, conv_ln_g, conv_ln_b, lru_conv_w, lru_conv_b,
           lru_wa, lru_ba, lru_wx, lru_bx, lru_lambda, sinks, w_out, band_bias, valid):
    u = x @ w_in
    o1 = 2 * D_CONV
    o2 = o1 + 2 * D_LRU
    o3 = o2 + D_ATTN
    o4 = o3 + KV_DIM
    u_conv, u_lru, q, k, v = jnp.split(u, [o1, o2, o3, o4], axis=-1)
    y_conv = _conv_module(u_conv, conv_dw_w, conv_dw_b, conv_ln_g, conv_ln_b)
    y_lru = _recurrent(u_lru, lru_conv_w, lru_conv_b, lru_wa, lru_ba, lru_wx, lru_bx, lru_lambda)
    y_attn = _swa(q, k, v, band_bias, valid, sinks)
    return jnp.concatenate([y_conv, y_lru, y_attn], axis=-1) @ w_out


def setup_inputs(seed: int = 0) -> dict:
    key = jax.random.key(seed)
    ks = jax.random.split(key, 24)
    f32 = jnp.float32
    nrm = lambda k, shape, scale: jax.random.normal(k, shape, f32) * scale
    x = nrm(ks[0], (BATCH, SEQ, D_MODEL), 1.0)
    rel_bias = nrm(ks[1], (REL_BUCKETS, N_Q_HEADS), 0.5)
    ln_g = 1.0 + nrm(ks[2], (DEPTH, 3, D_MODEL), 0.1)
    ln_b = nrm(ks[3], (DEPTH, 3, D_MODEL), 0.02)
    ffn_w_gate = nrm(ks[4], (DEPTH, 2, D_MODEL, D_FF), D_MODEL ** -0.5)
    ffn_w_up = nrm(ks[5], (DEPTH, 2, D_MODEL, D_FF), D_MODEL ** -0.5)
    ffn_w_down = nrm(ks[6], (DEPTH, 2, D_FF, D_MODEL), BETA * D_FF ** -0.5)
    w_in = nrm(ks[7], (DEPTH, D_MODEL, D_IN), D_MODEL ** -0.5)
    conv_dw_w = nrm(ks[8], (DEPTH, CONV_KERNEL, D_CONV), CONV_KERNEL ** -0.5)
    conv_dw_b = nrm(ks[9], (DEPTH, D_CONV), 0.02)
    conv_ln_g = 1.0 + nrm(ks[10], (DEPTH, D_CONV), 0.1)
    conv_ln_b = nrm(ks[11], (DEPTH, D_CONV), 0.02)
    lru_conv_w = nrm(ks[12], (DEPTH, LRU_CONV, D_LRU), LRU_CONV ** -0.5)
    lru_conv_b = nrm(ks[13], (DEPTH, D_LRU), 0.02)
    lru_wa = nrm(ks[14], (DEPTH, LRU_HEADS, LRU_HEAD_DIM, LRU_HEAD_DIM), LRU_HEAD_DIM ** -0.5)
    lru_ba = nrm(ks[15], (DEPTH, D_LRU), 0.02)
    lru_wx = nrm(ks[16], (DEPTH, LRU_HEADS, LRU_HEAD_DIM, LRU_HEAD_DIM), LRU_HEAD_DIM ** -0.5)
    lru_bx = nrm(ks[17], (DEPTH, D_LRU), 0.02)
    a_init = jax.random.uniform(ks[18], (DEPTH, D_LRU), f32, 0.9, 0.999)
    sig = a_init ** (1.0 / LRU_C)
    lru_lambda = jnp.log(sig) - jnp.log1p(-sig)
    attn_sinks = nrm(ks[19], (DEPTH, N_Q_HEADS), 0.5)
    w_out = nrm(ks[20], (DEPTH, D_MIX, D_MODEL), BETA * D_MIX ** -0.5)
    return {'x': x, 'rel_bias': rel_bias, 'ln_g': ln_g, 'ln_b': ln_b,
            'ffn_w_gate': ffn_w_gate, 'ffn_w_up': ffn_w_up, 'ffn_w_down': ffn_w_down,
            'w_in': w_in, 'conv_dw_w': conv_dw_w, 'conv_dw_b': conv_dw_b,
            'conv_ln_g': conv_ln_g, 'conv_ln_b': conv_ln_b,
            'lru_conv_w': lru_conv_w, 'lru_conv_b': lru_conv_b,
            'lru_wa': lru_wa, 'lru_ba': lru_ba, 'lru_wx': lru_wx, 'lru_bx': lru_bx,
            'lru_lambda': lru_lambda, 'attn_sinks': attn_sinks, 'w_out': w_out}


def reference(x, rel_bias, ln_g, ln_b, ffn_w_gate, ffn_w_up, ffn_w_down, w_in,
              conv_dw_w, conv_dw_b, conv_ln_g, conv_ln_b, lru_conv_w, lru_conv_b,
              lru_wa, lru_ba, lru_wx, lru_bx, lru_lambda, attn_sinks, w_out):
    band_bias, valid = _band_bias_and_mask(rel_bias, x.shape[1])
    for l in range(DEPTH):
        h = 0.5 * _swiglu(x, ffn_w_gate[l, 0], ffn_w_up[l, 0], ffn_w_down[l, 0])
        x = _layernorm(ALPHA * x + h, ln_g[l, 0], ln_b[l, 0])
        h = _mixer(x, w_in[l], conv_dw_w[l], conv_dw_b[l], conv_ln_g[l], conv_ln_b[l],
                   lru_conv_w[l], lru_conv_b[l], lru_wa[l], lru_ba[l], lru_wx[l], lru_bx[l],
                   lru_lambda[l], attn_sinks[l], w_out[l], band_bias, valid)
        x = _layernorm(ALPHA * x + h, ln_g[l, 1], ln_b[l, 1])
        h = 0.5 * _swiglu(x, ffn_w_gate[l, 1], ffn_w_up[l, 1], ffn_w_down[l, 1])
        x = _layernorm(ALPHA * x + h, ln_g[l, 2], ln_b[l, 2])
    return x
```

```python
import functools
import math

import jax
import jax.numpy as jnp
from jax import lax
from jax.experimental import pallas as pl
from jax.experimental.pallas import tpu as pltpu

D_MODEL = 1024
DEPTH = 2
D_CONV = 256
D_LRU = 256
D_ATTN = 512
CONV_KERNEL = 31
LRU_HEADS = 4
LRU_HEAD_DIM = 64
LRU_CONV = 4
LRU_C = 8.0
HEAD_DIM = 64
N_Q_HEADS = 8
N_KV_HEADS = 2
KV_DIM = 128
WINDOW = 128
BLOCK = 128
REL_BUCKETS = 32
REL_MAX_DIST = 128
D_FF = 2816
ALPHA = (2.0 * DEPTH) ** 0.25
LN_EPS = 1e-5
D_IN = 2 * D_CONV + 2 * D_LRU + D_ATTN + 2 * KV_DIM

LANES = 128
FF_CHUNK = 256
FFN_TILE = 512
SEQ_TILE = 256
CONV_HIST = 32
LRU_HIST = 8
VMEM_LIMIT_BYTES = 56 * 1024 * 1024

F32 = jnp.float32
BF16 = jnp.bfloat16


def _layernorm(z, g, b):
    mu = jnp.mean(z, axis=-1, keepdims=True)
    zc = z - mu
    var = jnp.mean(zc * zc, axis=-1, keepdims=True)
    return zc * lax.rsqrt(var + LN_EPS) * g + b


def _dot(a, b):
    return jnp.dot(a, b, preferred_element_type=F32)


def _dot_nt(a, b):
    return lax.dot_general(a, b, (((1,), (1,)), ((), ())), preferred_element_type=F32)


def _ffn_kernel(x_ref, wg_ref, wu_ref, wd_ref, g_ref, b_ref, o_ref, h_ref):
    x = x_ref[...]
    xb = x.astype(BF16)
    for c in range(D_FF // FF_CHUNK):
        sl = slice(c * FF_CHUNK, (c + 1) * FF_CHUNK)
        gate = _dot(xb, wg_ref[:, sl])
        up = _dot(xb, wu_ref[:, sl])
        h_ref[:, sl] = (gate * jax.nn.sigmoid(gate) * up).astype(BF16)
    y = _dot(h_ref[...], wd_ref[...])
    o_ref[...] = _layernorm(ALPHA * x + 0.5 * y, g_ref[...], b_ref[...])


def _resident(shape):
    return pl.BlockSpec(shape, lambda *_: (0,) * len(shape), pipeline_mode=pl.Buffered(1))


def _ffn_call(x2d, wg, wu, wd, g, b):
    n_tok = x2d.shape[0]
    return pl.pallas_call(
        _ffn_kernel,
        grid=(n_tok // FFN_TILE,),
        in_specs=[
            pl.BlockSpec((FFN_TILE, D_MODEL), lambda i: (i, 0)),
            _resident((D_MODEL, D_FF)),
            _resident((D_MODEL, D_FF)),
            _resident((D_FF, D_MODEL)),
            _resident((1, D_MODEL)),
            _resident((1, D_MODEL)),
        ],
        out_specs=pl.BlockSpec((FFN_TILE, D_MODEL), lambda i: (i, 0)),
        out_shape=jax.ShapeDtypeStruct((n_tok, D_MODEL), F32),
        scratch_shapes=[pltpu.VMEM((FFN_TILE, D_FF), BF16)],
        compiler_params=pltpu.CompilerParams(
            dimension_semantics=("arbitrary",), vmem_limit_bytes=VMEM_LIMIT_BYTES),
        name="ffn_ln",
    )(x2d, wg, wu, wd, g, b)


def _shift_rows(v, k, fill, row):
    return jnp.where(row >= k, pltpu.roll(v, k, 0), fill)


def _mixer_kernel(x_ref, win_ref, wout_ref, bias_ref,
                  cw_ref, cb_ref, cg_ref, cbeta_ref,
                  lw_ref, lb_ref, wa_ref, ba_ref, wx_ref, bx_ref, lam_ref,
                  sink_ref, g_ref, b_ref,
                  o_ref,
                  cbuf, xbuf, hcar, k_a, k_b, v0lo, v0hi, v1lo, v1hi, ycat):
    T = SEQ_TILE
    s_idx = pl.program_id(1)
    kv_bufs = (k_a, k_b, v0lo, v0hi, v1lo, v1hi)

    @pl.when(s_idx == 0)
    def _start_of_sequence():
        cbuf[0:CONV_HIST, :] = jnp.zeros((CONV_HIST, D_CONV), F32)
        xbuf[0:LRU_HIST, :] = jnp.zeros((LRU_HIST, D_LRU), F32)
        hcar[...] = jnp.zeros((1, D_LRU), F32)
        for buf in kv_bufs:
            buf[0:BLOCK, :] = jnp.zeros((BLOCK, KV_DIM), BF16)

    x = x_ref[0]
    u = _dot(x.astype(BF16), win_ref[...])

    yglu = u[:, 0:D_CONV] * jax.nn.sigmoid(u[:, D_CONV:2 * D_CONV])
    cbuf[CONV_HIST:CONV_HIST + T, :] = yglu
    off = CONV_HIST - (CONV_KERNEL - 1)
    acc = jnp.broadcast_to(cb_ref[...], (T, D_CONV))
    for j in range(CONV_KERNEL):
        acc = acc + cw_ref[j:j + 1, :] * cbuf[off + j:off + j + T, :]
    cbuf[0:CONV_HIST, :] = cbuf[T:T + CONV_HIST, :]
    yc = _layernorm(acc, cg_ref[...], cbeta_ref[...])
    ycat[:, 0:D_CONV] = (yc * jax.nn.sigmoid(yc)).astype(BF16)

    o1 = 2 * D_CONV
    xbuf[LRU_HIST:LRU_HIST + T, :] = u[:, o1:o1 + D_LRU]
    gb = u[:, o1 + D_LRU:o1 + 2 * D_LRU]
    off = LRU_HIST - (LRU_CONV - 1)
    xc = jnp.broadcast_to(lb_ref[...], (T, D_LRU))
    for j in range(LRU_CONV):
        xc = xc + lw_ref[j:j + 1, :] * xbuf[off + j:off + j + T, :]
    xbuf[0:LRU_HIST, :] = xbuf[T:T + LRU_HIST, :]
    xcb = xc.astype(BF16)
    r = jax.nn.sigmoid(_dot(xcb, wa_ref[...]) + ba_ref[...])
    i = jax.nn.sigmoid(_dot(xcb, wx_ref[...]) + bx_ref[...])
    lam = lam_ref[...]
    log_sig = -(jnp.maximum(-lam, 0.0) + jnp.log1p(jnp.exp(-jnp.abs(lam))))
    log_a = LRU_C * r * log_sig
    a = jnp.exp(log_a)
    mult = jnp.sqrt(-jnp.tanh(log_a) * (a * a + 1.0))
    bterm = mult * (i * xc)
    row = lax.broadcasted_iota(jnp.int32, (T, D_LRU), 0)
    k = 1
    while k < T:
        a_prev = _shift_rows(a, k, 1.0, row)
        b_prev = _shift_rows(bterm, k, 0.0, row)
        bterm = a * b_prev + bterm
        a = a * a_prev
        k *= 2
    h = a * hcar[...] + bterm
    hcar[...] = h[T - 1:T, :]
    ycat[:, D_CONV:D_CONV + D_LRU] = (h * jax.nn.gelu(gb)).astype(BF16)

    o2 = o1 + 2 * D_LRU
    o3 = o2 + D_ATTN
    kk = u[:, o3:o3 + KV_DIM]
    vv = u[:, o3 + KV_DIM:o3 + 2 * KV_DIM]
    lane = lax.broadcasted_iota(jnp.int32, (T, LANES), 1)
    lo = lane < HEAD_DIM
    k_rot = pltpu.roll(kk, HEAD_DIM, 1)
    v_rot = pltpu.roll(vv, HEAD_DIM, 1)
    zero = jnp.zeros((T, LANES), F32)
    new_rows = slice(BLOCK, BLOCK + T)
    k_a[new_rows, :] = kk.astype(BF16)
    k_b[new_rows, :] = k_rot.astype(BF16)
    v0lo[new_rows, :] = jnp.where(lo, vv, zero).astype(BF16)
    v0hi[new_rows, :] = jnp.where(lo, zero, v_rot).astype(BF16)
    v1lo[new_rows, :] = jnp.where(lo, v_rot, zero).astype(BF16)
    v1hi[new_rows, :] = jnp.where(lo, zero, vv).astype(BF16)

    qi = lax.broadcasted_iota(jnp.int32, (BLOCK, 2 * BLOCK), 0)
    kj = lax.broadcasted_iota(jnp.int32, (BLOCK, 2 * BLOCK), 1)
    dist = qi - kj + BLOCK
    in_window = (dist >= 0) & (dist < WINDOW)
    qlane_lo = lax.broadcasted_iota(jnp.int32, (BLOCK, LANES), 1) < HEAD_DIM
    qzero = jnp.zeros((BLOCK, LANES), F32)
    scale = HEAD_DIM ** -0.5
    for qb in range(T // BLOCK):
        r0 = qb * BLOCK
        band = slice(r0, r0 + 2 * BLOCK)
        if qb == 0:
            first_key = jnp.where(s_idx == 0, BLOCK, 0)
            valid = in_window & (kj >= first_key)
        else:
            valid = in_window
        for c in range(N_Q_HEADS // 2):
            grp = c // (N_Q_HEADS // 2 // N_KV_HEADS)
            qc = u[r0:r0 + BLOCK, o2 + c * LANES:o2 + (c + 1) * LANES]
            q_even = jnp.where(qlane_lo, qc, qzero).astype(BF16)
            q_odd = jnp.where(qlane_lo, qzero, qc).astype(BF16)
            k_even = (k_a if grp == 0 else k_b)[band, :]
            k_odd = (k_b if grp == 0 else k_a)[band, :]
            probs = []
            for hh, (qh, kh) in enumerate(((q_even, k_even), (q_odd, k_odd))):
                head = 2 * c + hh
                s = _dot_nt(qh, kh) * scale + bias_ref[head]
                s = jnp.where(valid, s, -1e30)
                sink = sink_ref[head]
                m = jnp.maximum(jnp.max(s, axis=-1, keepdims=True), sink)
                e = jnp.exp(s - m)
                den = jnp.sum(e, axis=-1, keepdims=True) + jnp.exp(sink - m)
                probs.append((e / den).astype(BF16))
            v_even = (v0lo if grp == 0 else v1lo)[band, :]
            v_odd = (v0hi if grp == 0 else v1hi)[band, :]
            o_pair = _dot(probs[0], v_even) + _dot(probs[1], v_odd)
            col = D_CONV + D_LRU + c * LANES
            ycat[r0:r0 + BLOCK, col:col + LANES] = o_pair.astype(BF16)
    for buf in kv_bufs:
        buf[0:BLOCK, :] = buf[T:T + BLOCK, :]

    hproj = _dot(ycat[...], wout_ref[...])
    o_ref[0] = _layernorm(ALPHA * x + hproj, g_ref[...], b_ref[...])


def _mixer_call(x, w_in, w_out, bias, cw, cb, cg, cbeta, lw, lb, wa, ba, wx, bx, lam, sinks, g, b):
    B, S, _ = x.shape
    T = SEQ_TILE
    x_spec = pl.BlockSpec((1, T, D_MODEL), lambda bi, si: (bi, si, 0))
    in_specs = [
        x_spec,
        _resident((D_MODEL, D_IN)),
        _resident((D_MODEL, D_MODEL)),
        _resident((N_Q_HEADS, BLOCK, 2 * BLOCK)),
        _resident((CONV_KERNEL, D_CONV)), _resident((1, D_CONV)), _resident((1, D_CONV)), _resident((1, D_CONV)),
        _resident((LRU_CONV, D_LRU)), _resident((1, D_LRU)),
        _resident((D_LRU, D_LRU)), _resident((1, D_LRU)),
        _resident((D_LRU, D_LRU)), _resident((1, D_LRU)),
        _resident((1, D_LRU)),
        pl.BlockSpec(memory_space=pltpu.SMEM),
        _resident((1, D_MODEL)), _resident((1, D_MODEL)),
    ]
    scratch = [
        pltpu.VMEM((CONV_HIST + T, D_CONV), F32),
        pltpu.VMEM((LRU_HIST + T, D_LRU), F32),
        pltpu.VMEM((1, D_LRU), F32),
    ] + [pltpu.VMEM((BLOCK + T, KV_DIM), BF16) for _ in range(6)] + [
        pltpu.VMEM((T, D_MODEL), BF16),
    ]
    return pl.pallas_call(
        _mixer_kernel,
        grid=(B, S // T),
        in_specs=in_specs,
        out_specs=x_spec,
        out_shape=jax.ShapeDtypeStruct(x.shape, F32),
        scratch_shapes=scratch,
        compiler_params=pltpu.CompilerParams(
            dimension_semantics=("arbitrary", "arbitrary"), vmem_limit_bytes=VMEM_LIMIT_BYTES),
        name="mixer_ln",
    )(x, w_in, w_out, bias, cw, cb, cg, cbeta, lw, lb, wa, ba, wx, bx, lam, sinks, g, b)


def _rel_bucket(dist):
    max_exact = REL_BUCKETS // 2
    large = max_exact + (jnp.log(jnp.maximum(dist, 1).astype(F32) / max_exact)
                         / math.log(REL_MAX_DIST / max_exact) * (REL_BUCKETS - max_exact)).astype(jnp.int32)
    return jnp.where(dist < max_exact, dist, jnp.minimum(large, REL_BUCKETS - 1))


def _band_bias(rel_bias):
    qi = jnp.arange(BLOCK)[:, None]
    kj = jnp.arange(2 * BLOCK)[None, :]
    bucket = _rel_bucket(jnp.maximum(qi - kj + BLOCK, 0))
    return jnp.transpose(rel_bias[bucket].astype(F32), (2, 0, 1))


def _block_diag(w):
    h, d, _ = w.shape
    eye = jnp.eye(h, dtype=w.dtype)
    return (eye[:, None, :, None] * w[:, :, None, :]).reshape(h * d, h * d)


def kernel(x, rel_bias, ln_g, ln_b, ffn_w_gate, ffn_w_up, ffn_w_down, w_in, conv_dw_w, conv_dw_b, conv_ln_g, conv_ln_b, lru_conv_w, lru_conv_b, lru_wa, lru_ba, lru_wx, lru_bx, lru_lambda, attn_sinks, w_out):
    B, S, D = x.shape
    assert D == D_MODEL and S % SEQ_TILE == 0 and (B * S) % FFN_TILE == 0
    bias = _band_bias(rel_bias)
    row = lambda v: v.reshape(1, -1)
    for l in range(DEPTH):
        def ffn(x, j):
            y = _ffn_call(x.reshape(B * S, D),
                          ffn_w_gate[l, j].astype(BF16), ffn_w_up[l, j].astype(BF16),
                          ffn_w_down[l, j].astype(BF16), row(ln_g[l, 2 * j]), row(ln_b[l, 2 * j]))
            return y.reshape(B, S, D)

        x = ffn(x, 0)
        x = _mixer_call(
            x, w_in[l].astype(BF16), w_out[l].astype(BF16), bias,
            conv_dw_w[l], row(conv_dw_b[l]), row(conv_ln_g[l]), row(conv_ln_b[l]),
            lru_conv_w[l], row(lru_conv_b[l]),
            _block_diag(lru_wa[l]).astype(BF16), row(lru_ba[l]),
            _block_diag(lru_wx[l]).astype(BF16), row(lru_bx[l]),
            row(lru_lambda[l]), attn_sinks[l], row(ln_g[l, 1]), row(ln_b[l, 1]))
        x = ffn(x, 1)
    return x
```

```python
import functools
import math

import jax
import jax.numpy as jnp
import numpy as np
from jax import lax
from jax.experimental import pallas as pl
from jax.experimental.pallas import tpu as pltpu

D_MODEL = 1024
DEPTH = 2
D_CONV = 256
D_LRU = 256
D_ATTN = 512
CONV_KERNEL = 31
LRU_HEADS = 4
LRU_HEAD_DIM = 64
LRU_CONV = 4
LRU_C = 8.0
HEAD_DIM = 64
N_Q_HEADS = 8
N_KV_HEADS = 2
KV_DIM = 128
WINDOW = 128
BLOCK = 128
REL_BUCKETS = 32
REL_MAX_DIST = 128
D_FF = 2816
ALPHA = (2.0 * DEPTH) ** 0.25
LN_EPS = 1e-5
D_IN = 2 * D_CONV + 2 * D_LRU + D_ATTN + 2 * KV_DIM

LANES = 128
FF_CHUNK = 256
FFN_TILE = 512
SEQ_TILE = 256
CONV_HIST = 32
LRU_HIST = 8
VMEM_LIMIT_BYTES = 56 * 1024 * 1024

F32 = jnp.float32
BF16 = jnp.bfloat16


def _layernorm(z, g, b):
    mu = jnp.mean(z, axis=-1, keepdims=True)
    zc = z - mu
    var = jnp.mean(zc * zc, axis=-1, keepdims=True)
    return zc * lax.rsqrt(var + LN_EPS) * g + b


def _dot(a, b):
    return jnp.dot(a, b, preferred_element_type=F32)


def _dot_nt(a, b):
    return lax.dot_general(a, b, (((1,), (1,)), ((), ())), preferred_element_type=F32)


def _resident(shape, lead=()):
    block = (None,) * len(lead) + tuple(shape)
    index = tuple(lead) + (0,) * len(shape)
    return pl.BlockSpec(block, lambda *_: index, pipeline_mode=pl.Buffered(1))


def _ffn_kernel(x_ref, wg_ref, wu_ref, wd_ref, g_ref, b_ref, o_ref, h_ref):
    x = x_ref[...]
    xb = x.astype(BF16)
    for c in range(D_FF // FF_CHUNK):
        sl = slice(c * FF_CHUNK, (c + 1) * FF_CHUNK)
        gate = _dot(xb, wg_ref[:, sl])
        up = _dot(xb, wu_ref[:, sl])
        h_ref[:, sl] = (gate * jax.nn.sigmoid(gate) * up).astype(BF16)
    y = _dot(h_ref[...], wd_ref[...])
    o_ref[...] = _layernorm(ALPHA * x + 0.5 * y, g_ref[...], b_ref[...])


def _ffn_call(x2d, wg, wu, wd, ln_g, ln_b, layer, half):
    n_tok = x2d.shape[0]
    return pl.pallas_call(
        _ffn_kernel,
        grid=(n_tok // FFN_TILE,),
        in_specs=[
            pl.BlockSpec((FFN_TILE, D_MODEL), lambda i: (i, 0)),
            _resident((D_MODEL, D_FF), (layer, half)),
            _resident((D_MODEL, D_FF), (layer, half)),
            _resident((D_FF, D_MODEL), (layer, half)),
            _resident((1, D_MODEL), (layer, 2 * half)),
            _resident((1, D_MODEL), (layer, 2 * half)),
        ],
        out_specs=pl.BlockSpec((FFN_TILE, D_MODEL), lambda i: (i, 0)),
        out_shape=jax.ShapeDtypeStruct((n_tok, D_MODEL), F32),
        scratch_shapes=[pltpu.VMEM((FFN_TILE, D_FF), BF16)],
        compiler_params=pltpu.CompilerParams(
            dimension_semantics=("arbitrary",), vmem_limit_bytes=VMEM_LIMIT_BYTES),
        name="ffn_ln",
    )(x2d, wg, wu, wd, ln_g, ln_b)


def _shift_rows(v, k, fill, row):
    return jnp.where(row >= k, pltpu.roll(v, k, 0), fill)


def _mixer_kernel(layer,
                  x_ref, win_ref, wout_ref, bucket_ref, relb_ref,
                  cw_ref, cb_ref, cg_ref, cbeta_ref,
                  lw_ref, lb_ref, wa_ref, ba_ref, wx_ref, bx_ref, lam_ref,
                  sink_ref, g_ref, b_ref,
                  o_ref,
                  bias_tab, cbuf, xbuf, hcar, k_a, k_b, v0lo, v0hi, v1lo, v1hi, ycat):
    T = SEQ_TILE
    s_idx = pl.program_id(1)
    kv_bufs = (k_a, k_b, v0lo, v0hi, v1lo, v1hi)

    @pl.when((pl.program_id(0) == 0) & (s_idx == 0))
    def _build_bias_table():
        bucket = bucket_ref[...]
        for head in range(N_Q_HEADS):
            tab = jnp.zeros((BLOCK, 2 * BLOCK), F32)
            for bkt in range(REL_BUCKETS):
                tab = jnp.where(bucket == bkt, relb_ref[bkt, head], tab)
            bias_tab[head] = tab

    @pl.when(s_idx == 0)
    def _start_of_sequence():
        cbuf[0:CONV_HIST, :] = jnp.zeros((CONV_HIST, D_CONV), F32)
        xbuf[0:LRU_HIST, :] = jnp.zeros((LRU_HIST, D_LRU), F32)
        hcar[...] = jnp.zeros((1, D_LRU), F32)
        for buf in kv_bufs:
            buf[0:BLOCK, :] = jnp.zeros((BLOCK, KV_DIM), BF16)

    x = x_ref[0]
    u = _dot(x.astype(BF16), win_ref[...])

    yglu = u[:, 0:D_CONV] * jax.nn.sigmoid(u[:, D_CONV:2 * D_CONV])
    cbuf[CONV_HIST:CONV_HIST + T, :] = yglu
    off = CONV_HIST - (CONV_KERNEL - 1)
    acc = jnp.broadcast_to(cb_ref[...], (T, D_CONV))
    for j in range(CONV_KERNEL):
        acc = acc + cw_ref[j:j + 1, :] * cbuf[off + j:off + j + T, :]
    cbuf[0:CONV_HIST, :] = cbuf[T:T + CONV_HIST, :]
    yc = _layernorm(acc, cg_ref[...], cbeta_ref[...])
    ycat[:, 0:D_CONV] = (yc * jax.nn.sigmoid(yc)).astype(BF16)

    o1 = 2 * D_CONV
    xbuf[LRU_HIST:LRU_HIST + T, :] = u[:, o1:o1 + D_LRU]
    gb = u[:, o1 + D_LRU:o1 + 2 * D_LRU]
    off = LRU_HIST - (LRU_CONV - 1)
    xc = jnp.broadcast_to(lb_ref[...], (T, D_LRU))
    for j in range(LRU_CONV):
        xc = xc + lw_ref[j:j + 1, :] * xbuf[off + j:off + j + T, :]
    xbuf[0:LRU_HIST, :] = xbuf[T:T + LRU_HIST, :]
    xcb = xc.astype(BF16)
    r = jax.nn.sigmoid(_dot(xcb, wa_ref[...]) + ba_ref[...])
    i = jax.nn.sigmoid(_dot(xcb, wx_ref[...]) + bx_ref[...])
    lam = lam_ref[...]
    log_sig = -(jnp.maximum(-lam, 0.0) + jnp.log1p(jnp.exp(-jnp.abs(lam))))
    log_a = LRU_C * r * log_sig
    a = jnp.exp(log_a)
    mult = jnp.sqrt(-jnp.tanh(log_a) * (a * a + 1.0))
    bterm = mult * (i * xc)
    row = lax.broadcasted_iota(jnp.int32, (T, D_LRU), 0)
    k = 1
    while k < T:
        a_prev = _shift_rows(a, k, 1.0, row)
        b_prev = _shift_rows(bterm, k, 0.0, row)
        bterm = a * b_prev + bterm
        a = a * a_prev
        k *= 2
    h = a * hcar[...] + bterm
    hcar[...] = h[T - 1:T, :]
    ycat[:, D_CONV:D_CONV + D_LRU] = (h * jax.nn.gelu(gb)).astype(BF16)

    o2 = o1 + 2 * D_LRU
    o3 = o2 + D_ATTN
    kk = u[:, o3:o3 + KV_DIM]
    vv = u[:, o3 + KV_DIM:o3 + 2 * KV_DIM]
    lane = lax.broadcasted_iota(jnp.int32, (T, LANES), 1)
    lo = lane < HEAD_DIM
    k_rot = pltpu.roll(kk, HEAD_DIM, 1)
    v_rot = pltpu.roll(vv, HEAD_DIM, 1)
    zero = jnp.zeros((T, LANES), F32)
    new_rows = slice(BLOCK, BLOCK + T)
    k_a[new_rows, :] = kk.astype(BF16)
    k_b[new_rows, :] = k_rot.astype(BF16)
    v0lo[new_rows, :] = jnp.where(lo, vv, zero).astype(BF16)
    v0hi[new_rows, :] = jnp.where(lo, zero, v_rot).astype(BF16)
    v1lo[new_rows, :] = jnp.where(lo, v_rot, zero).astype(BF16)
    v1hi[new_rows, :] = jnp.where(lo, zero, vv).astype(BF16)

    qi = lax.broadcasted_iota(jnp.int32, (BLOCK, 2 * BLOCK), 0)
    kj = lax.broadcasted_iota(jnp.int32, (BLOCK, 2 * BLOCK), 1)
    dist = qi - kj + BLOCK
    in_window = (dist >= 0) & (dist < WINDOW)
    qlane_lo = lax.broadcasted_iota(jnp.int32, (BLOCK, LANES), 1) < HEAD_DIM
    qzero = jnp.zeros((BLOCK, LANES), F32)
    scale = HEAD_DIM ** -0.5
    for qb in range(T // BLOCK):
        r0 = qb * BLOCK
        band = slice(r0, r0 + 2 * BLOCK)
        if qb == 0:
            first_key = jnp.where(s_idx == 0, BLOCK, 0)
            valid = in_window & (kj >= first_key)
        else:
            valid = in_window
        for c in range(N_Q_HEADS // 2):
            grp = c // (N_Q_HEADS // 2 // N_KV_HEADS)
            qc = u[r0:r0 + BLOCK, o2 + c * LANES:o2 + (c + 1) * LANES]
            q_even = jnp.where(qlane_lo, qc, qzero).astype(BF16)
            q_odd = jnp.where(qlane_lo, qzero, qc).astype(BF16)
            k_even = (k_a if grp == 0 else k_b)[band, :]
            k_odd = (k_b if grp == 0 else k_a)[band, :]
            probs = []
            for hh, (qh, kh) in enumerate(((q_even, k_even), (q_odd, k_odd))):
                head = 2 * c + hh
                s = _dot_nt(qh, kh) * scale + bias_tab[head]
                s = jnp.where(valid, s, -1e30)
                sink = sink_ref[layer, head]
                m = jnp.maximum(jnp.max(s, axis=-1, keepdims=True), sink)
                e = jnp.exp(s - m)
                den = jnp.sum(e, axis=-1, keepdims=True) + jnp.exp(sink - m)
                probs.append((e / den).astype(BF16))
            v_even = (v0lo if grp == 0 else v1lo)[band, :]
            v_odd = (v0hi if grp == 0 else v1hi)[band, :]
            o_pair = _dot(probs[0], v_even) + _dot(probs[1], v_odd)
            col = D_CONV + D_LRU + c * LANES
            ycat[r0:r0 + BLOCK, col:col + LANES] = o_pair.astype(BF16)
    for buf in kv_bufs:
        buf[0:BLOCK, :] = buf[T:T + BLOCK, :]

    hproj = _dot(ycat[...], wout_ref[...])
    o_ref[0] = _layernorm(ALPHA * x + hproj, g_ref[...], b_ref[...])


def _mixer_call(x, layer, w_in, w_out, bucket, rel_bias, cw, cb, cg, cbeta, lw, lb, wa, ba, wx, bx, lam,
                sinks, ln_g, ln_b):
    B, S, _ = x.shape
    T = SEQ_TILE
    x_spec = pl.BlockSpec((1, T, D_MODEL), lambda bi, si: (bi, si, 0))
    lay = (layer,)
    smem = pl.BlockSpec(memory_space=pltpu.SMEM)
    in_specs = [
        x_spec,
        _resident((D_MODEL, D_IN), lay),
        _resident((D_MODEL, D_MODEL), lay),
        _resident((BLOCK, 2 * BLOCK)),
        smem,
        _resident((CONV_KERNEL, D_CONV), lay), _resident((1, D_CONV), lay),
        _resident((1, D_CONV), lay), _resident((1, D_CONV), lay),
        _resident((LRU_CONV, D_LRU), lay), _resident((1, D_LRU), lay),
        _resident((D_LRU, D_LRU), lay), _resident((1, D_LRU), lay),
        _resident((D_LRU, D_LRU), lay), _resident((1, D_LRU), lay),
        _resident((1, D_LRU), lay),
        smem,
        _resident((1, D_MODEL), (layer, 1)), _resident((1, D_MODEL), (layer, 1)),
    ]
    scratch = [
        pltpu.VMEM((N_Q_HEADS, BLOCK, 2 * BLOCK), F32),
        pltpu.VMEM((CONV_HIST + T, D_CONV), F32),
        pltpu.VMEM((LRU_HIST + T, D_LRU), F32),
        pltpu.VMEM((1, D_LRU), F32),
    ] + [pltpu.VMEM((BLOCK + T, KV_DIM), BF16) for _ in range(6)] + [
        pltpu.VMEM((T, D_MODEL), BF16),
    ]
    return pl.pallas_call(
        functools.partial(_mixer_kernel, layer),
        grid=(B, S // T),
        in_specs=in_specs,
        out_specs=x_spec,
        out_shape=jax.ShapeDtypeStruct(x.shape, F32),
        scratch_shapes=scratch,
        compiler_params=pltpu.CompilerParams(
            dimension_semantics=("arbitrary", "arbitrary"), vmem_limit_bytes=VMEM_LIMIT_BYTES),
        name="mixer_ln",
    )(x, w_in, w_out, bucket, rel_bias, cw, cb, cg, cbeta, lw, lb, wa, ba, wx, bx, lam, sinks, ln_g, ln_b)


def _bucket_table():
    qi = np.arange(BLOCK)[:, None]
    kj = np.arange(2 * BLOCK)[None, :]
    dist = np.maximum(qi - kj + BLOCK, 0)
    max_exact = REL_BUCKETS // 2
    ratio = np.log(np.maximum(dist, 1).astype(np.float32) / np.float32(max_exact)) / np.float32(
        math.log(REL_MAX_DIST / max_exact))
    large = max_exact + (ratio * np.float32(REL_BUCKETS - max_exact)).astype(np.int32)
    return np.where(dist < max_exact, dist, np.minimum(large, REL_BUCKETS - 1)).astype(np.int32)


def _block_diag(w):
    l, h, d, _ = w.shape
    eye = jnp.eye(h, dtype=w.dtype)
    return (eye[None, :, None, :, None] * w[:, :, :, None, :]).reshape(l, h * d, h * d)


def kernel(x, rel_bias, ln_g, ln_b, ffn_w_gate, ffn_w_up, ffn_w_down, w_in, conv_dw_w, conv_dw_b, conv_ln_g, conv_ln_b, lru_conv_w, lru_conv_b, lru_wa, lru_ba, lru_wx, lru_bx, lru_lambda, attn_sinks, w_out):
    B, S, D = x.shape
    assert D == D_MODEL and S % SEQ_TILE == 0 and (B * S) % FFN_TILE == 0
    rows = lambda v: v.reshape(v.shape[:-1] + (1, v.shape[-1]))
    wg, wu, wd = ffn_w_gate.astype(BF16), ffn_w_up.astype(BF16), ffn_w_down.astype(BF16)
    w_in_b, w_out_b = w_in.astype(BF16), w_out.astype(BF16)
    wa, wx = _block_diag(lru_wa).astype(BF16), _block_diag(lru_wx).astype(BF16)
    g4, b4 = rows(ln_g), rows(ln_b)
    bucket = _bucket_table()
    for l in range(DEPTH):
        x = _ffn_call(x.reshape(B * S, D), wg, wu, wd, g4, b4, l, 0).reshape(B, S, D)
        x = _mixer_call(
            x, l, w_in_b, w_out_b, bucket, rel_bias,
            conv_dw_w, rows(conv_dw_b), rows(conv_ln_g), rows(conv_ln_b),
            lru_conv_w, rows(lru_conv_b), wa, rows(lru_ba), wx, rows(lru_bx), rows(lru_lambda),
            attn_sinks, g4, b4)
        x = _ffn_call(x.reshape(B * S, D), wg, wu, wd, g4, b4, l, 1).reshape(B, S, D)
    return x
```

```python
import functools
import math

import jax
import jax.numpy as jnp
import numpy as np
from jax import lax
from jax.experimental import pallas as pl
from jax.experimental.pallas import tpu as pltpu

D_MODEL = 1024
DEPTH = 2
D_CONV = 256
D_LRU = 256
D_ATTN = 512
CONV_KERNEL = 31
LRU_HEADS = 4
LRU_HEAD_DIM = 64
LRU_CONV = 4
LRU_C = 8.0
HEAD_DIM = 64
N_Q_HEADS = 8
N_KV_HEADS = 2
KV_DIM = 128
WINDOW = 128
BLOCK = 128
REL_BUCKETS = 32
REL_MAX_DIST = 128
D_FF = 2816
ALPHA = (2.0 * DEPTH) ** 0.25
LN_EPS = 1e-5
D_IN = 2 * D_CONV + 2 * D_LRU + D_ATTN + 2 * KV_DIM

LANES = 128
SUBLANES = 8
CONV_ROWS = 64
FF_CHUNK = 256
FFN_TILE = 512
SEQ_TILE = 256
CONV_HIST = 32
LRU_HIST = 8
VMEM_LIMIT_BYTES = 56 * 1024 * 1024

F32 = jnp.float32
BF16 = jnp.bfloat16


def _layernorm(z, g, b):
    mu = jnp.mean(z, axis=-1, keepdims=True)
    zc = z - mu
    var = jnp.mean(zc * zc, axis=-1, keepdims=True)
    return zc * lax.rsqrt(var + LN_EPS) * g + b


def _dot(a, b):
    return jnp.dot(a, b, preferred_element_type=F32)


def _dot_nt(a, b):
    return lax.dot_general(a, b, (((1,), (1,)), ((), ())), preferred_element_type=F32)


def _resident(shape, lead=()):
    block = (None,) * len(lead) + tuple(shape)
    index = tuple(lead) + (0,) * len(shape)
    return pl.BlockSpec(block, lambda *_: index, pipeline_mode=pl.Buffered(1))


def _ffn_kernel(x_ref, wg_ref, wu_ref, wd_ref, g_ref, b_ref, o_ref, h_ref):
    x = x_ref[...]
    xb = x.astype(BF16)
    for c in range(D_FF // FF_CHUNK):
        sl = slice(c * FF_CHUNK, (c + 1) * FF_CHUNK)
        gate = _dot(xb, wg_ref[:, sl])
        up = _dot(xb, wu_ref[:, sl])
        h_ref[:, sl] = (gate * jax.nn.sigmoid(gate) * up).astype(BF16)
    y = _dot(h_ref[...], wd_ref[...])
    o_ref[...] = _layernorm(ALPHA * x + 0.5 * y, g_ref[...], b_ref[...])


def _ffn_call(x2d, wg, wu, wd, ln_g, ln_b, layer, half):
    n_tok = x2d.shape[0]
    return pl.pallas_call(
        _ffn_kernel,
        grid=(n_tok // FFN_TILE,),
        in_specs=[
            pl.BlockSpec((FFN_TILE, D_MODEL), lambda i: (i, 0)),
            _resident((D_MODEL, D_FF), (layer, half)),
            _resident((D_MODEL, D_FF), (layer, half)),
            _resident((D_FF, D_MODEL), (layer, half)),
            _resident((1, D_MODEL), (layer, 2 * half)),
            _resident((1, D_MODEL), (layer, 2 * half)),
        ],
        out_specs=pl.BlockSpec((FFN_TILE, D_MODEL), lambda i: (i, 0)),
        out_shape=jax.ShapeDtypeStruct((n_tok, D_MODEL), F32),
        scratch_shapes=[pltpu.VMEM((FFN_TILE, D_FF), BF16)],
        compiler_params=pltpu.CompilerParams(
            dimension_semantics=("arbitrary",), vmem_limit_bytes=VMEM_LIMIT_BYTES),
        name="ffn_ln",
    )(x2d, wg, wu, wd, ln_g, ln_b)


def _mixer_kernel(layer,
                  x_ref, win_ref, wout_ref, bucket_ref, relb_ref,
                  cw_ref, cb_ref, cg_ref, cbeta_ref,
                  lw_ref, lb_ref, wa_ref, ba_ref, wx_ref, bx_ref, lam_ref,
                  sink_ref, g_ref, b_ref,
                  o_ref,
                  bias_tab, chist, cshift, xbuf, hcar, k_a, k_b, v0lo, v0hi, v1lo, v1hi, ycat):
    T = SEQ_TILE
    s_idx = pl.program_id(1)
    kv_bufs = (k_a, k_b, v0lo, v0hi, v1lo, v1hi)

    @pl.when((pl.program_id(0) == 0) & (s_idx == 0))
    def _build_bias_table():
        bucket = bucket_ref[...]
        for head in range(N_Q_HEADS):
            tab = jnp.zeros((BLOCK, 2 * BLOCK), F32)
            for bkt in range(REL_BUCKETS):
                tab = jnp.where(bucket == bkt, relb_ref[bkt, head], tab)
            bias_tab[head] = tab

    @pl.when(s_idx == 0)
    def _start_of_sequence():
        chist[...] = jnp.zeros((CONV_HIST, D_CONV), F32)
        xbuf[0:LRU_HIST, :] = jnp.zeros((LRU_HIST, D_LRU), F32)
        hcar[...] = jnp.zeros((1, D_LRU), F32)
        for buf in kv_bufs:
            buf[0:BLOCK, :] = jnp.zeros((BLOCK, KV_DIM), BF16)

    x = x_ref[0]
    u = _dot(x.astype(BF16), win_ref[...])

    yglu = u[:, 0:D_CONV] * jax.nn.sigmoid(u[:, D_CONV:2 * D_CONV])
    n_rows = CONV_HIST + T
    whole = jnp.concatenate([chist[...], yglu], axis=0)
    cshift[0] = whole
    for r in range(1, SUBLANES):
        cshift[r] = pltpu.roll(whole, n_rows - r, 0)
    chist[...] = yglu[T - CONV_HIST:T, :]
    off = CONV_HIST - (CONV_KERNEL - 1)
    for rb in range(T // CONV_ROWS):
        acc = jnp.broadcast_to(cb_ref[...], (CONV_ROWS, D_CONV))
        for j in range(CONV_KERNEL):
            r = (off + j) % SUBLANES
            start = rb * CONV_ROWS + off + j - r
            acc = acc + cw_ref[j:j + 1, :] * cshift[r, start:start + CONV_ROWS, :]
        yc = _layernorm(acc, cg_ref[...], cbeta_ref[...])
        ycat[rb * CONV_ROWS:(rb + 1) * CONV_ROWS, 0:D_CONV] = (yc * jax.nn.sigmoid(yc)).astype(BF16)

    o1 = 2 * D_CONV
    xbuf[LRU_HIST:LRU_HIST + T, :] = u[:, o1:o1 + D_LRU]
    gb = u[:, o1 + D_LRU:o1 + 2 * D_LRU]
    off = LRU_HIST - (LRU_CONV - 1)
    xc = jnp.broadcast_to(lb_ref[...], (T, D_LRU))
    for j in range(LRU_CONV):
        xc = xc + lw_ref[j:j + 1, :] * xbuf[off + j:off + j + T, :]
    xbuf[0:LRU_HIST, :] = xbuf[T:T + LRU_HIST, :]
    xcb = xc.astype(BF16)
    r = jax.nn.sigmoid(_dot(xcb, wa_ref[...]) + ba_ref[...])
    i = jax.nn.sigmoid(_dot(xcb, wx_ref[...]) + bx_ref[...])
    lam = lam_ref[...]
    log_sig = -(jnp.maximum(-lam, 0.0) + jnp.log1p(jnp.exp(-jnp.abs(lam))))
    log_a = LRU_C * r * log_sig
    a = jnp.exp(log_a)
    mult = jnp.sqrt(-jnp.tanh(log_a) * (a * a + 1.0))
    bterm = mult * (i * xc)
    n_grp = T // SUBLANES
    a3 = a.reshape(n_grp, SUBLANES, D_LRU)
    b3 = bterm.reshape(n_grp, SUBLANES, D_LRU)
    sub = lax.broadcasted_iota(jnp.int32, (n_grp, SUBLANES, D_LRU), 1)
    k = 1
    while k < SUBLANES:
        a_prev = jnp.where(sub >= k, pltpu.roll(a3, k, 1), 1.0)
        b_prev = jnp.where(sub >= k, pltpu.roll(b3, k, 1), 0.0)
        b3 = a3 * b_prev + b3
        a3 = a3 * a_prev
        k *= 2
    carry = hcar[...]
    h_groups = []
    for grp_i in range(n_grp):
        h_grp = a3[grp_i] * carry + b3[grp_i]
        carry = h_grp[SUBLANES - 1:SUBLANES, :]
        h_groups.append(h_grp)
    hcar[...] = carry
    h = jnp.concatenate(h_groups, axis=0)
    ycat[:, D_CONV:D_CONV + D_LRU] = (h * jax.nn.gelu(gb)).astype(BF16)

    o2 = o1 + 2 * D_LRU
    o3 = o2 + D_ATTN
    kk = u[:, o3:o3 + KV_DIM]
    vv = u[:, o3 + KV_DIM:o3 + 2 * KV_DIM]
    lane = lax.broadcasted_iota(jnp.int32, (T, LANES), 1)
    lo = lane < HEAD_DIM
    k_rot = pltpu.roll(kk, HEAD_DIM, 1)
    v_rot = pltpu.roll(vv, HEAD_DIM, 1)
    zero = jnp.zeros((T, LANES), F32)
    new_rows = slice(BLOCK, BLOCK + T)
    k_a[new_rows, :] = kk.astype(BF16)
    k_b[new_rows, :] = k_rot.astype(BF16)
    v0lo[new_rows, :] = jnp.where(lo, vv, zero).astype(BF16)
    v0hi[new_rows, :] = jnp.where(lo, zero, v_rot).astype(BF16)
    v1lo[new_rows, :] = jnp.where(lo, v_rot, zero).astype(BF16)
    v1hi[new_rows, :] = jnp.where(lo, zero, vv).astype(BF16)

    qi = lax.broadcasted_iota(jnp.int32, (BLOCK, 2 * BLOCK), 0)
    kj = lax.broadcasted_iota(jnp.int32, (BLOCK, 2 * BLOCK), 1)
    dist = qi - kj + BLOCK
    in_window = (dist >= 0) & (dist < WINDOW)
    qlane_lo = lax.broadcasted_iota(jnp.int32, (BLOCK, LANES), 1) < HEAD_DIM
    qzero = jnp.zeros((BLOCK, LANES), F32)
    scale = HEAD_DIM ** -0.5
    for qb in range(T // BLOCK):
        r0 = qb * BLOCK
        band = slice(r0, r0 + 2 * BLOCK)
        if qb == 0:
            first_key = jnp.where(s_idx == 0, BLOCK, 0)
            valid = in_window & (kj >= first_key)
        else:
            valid = in_window
        for c in range(N_Q_HEADS // 2):
            grp = c // (N_Q_HEADS // 2 // N_KV_HEADS)
            qc = u[r0:r0 + BLOCK, o2 + c * LANES:o2 + (c + 1) * LANES] * scale
            q_even = jnp.where(qlane_lo, qc, qzero).astype(BF16)
            q_odd = jnp.where(qlane_lo, qzero, qc).astype(BF16)
            k_even = (k_a if grp == 0 else k_b)[band, :]
            k_odd = (k_b if grp == 0 else k_a)[band, :]
            probs = []
            for hh, (qh, kh) in enumerate(((q_even, k_even), (q_odd, k_odd))):
                head = 2 * c + hh
                s = _dot_nt(qh, kh) + bias_tab[head]
                s = jnp.where(valid, s, -1e30)
                sink = sink_ref[layer, head]
                m = jnp.maximum(jnp.max(s, axis=-1, keepdims=True), sink)
                e = jnp.exp(s - m)
                den = jnp.sum(e, axis=-1, keepdims=True) + jnp.exp(sink - m)
                probs.append((e / den).astype(BF16))
            v_even = (v0lo if grp == 0 else v1lo)[band, :]
            v_odd = (v0hi if grp == 0 else v1hi)[band, :]
            o_pair = _dot(probs[0], v_even) + _dot(probs[1], v_odd)
            col = D_CONV + D_LRU + c * LANES
            ycat[r0:r0 + BLOCK, col:col + LANES] = o_pair.astype(BF16)
    for buf in kv_bufs:
        buf[0:BLOCK, :] = buf[T:T + BLOCK, :]

    hproj = _dot(ycat[...], wout_ref[...])
    o_ref[0] = _layernorm(ALPHA * x + hproj, g_ref[...], b_ref[...])


def _mixer_call(x, layer, w_in, w_out, bucket, rel_bias, cw, cb, cg, cbeta, lw, lb, wa, ba, wx, bx, lam,
                sinks, ln_g, ln_b):
    B, S, _ = x.shape
    T = SEQ_TILE
    x_spec = pl.BlockSpec((1, T, D_MODEL), lambda bi, si: (bi, si, 0))
    lay = (layer,)
    smem = pl.BlockSpec(memory_space=pltpu.SMEM)
    in_specs = [
        x_spec,
        _resident((D_MODEL, D_IN), lay),
        _resident((D_MODEL, D_MODEL), lay),
        _resident((BLOCK, 2 * BLOCK)),
        smem,
        _resident((CONV_KERNEL, D_CONV), lay), _resident((1, D_CONV), lay),
        _resident((1, D_CONV), lay), _resident((1, D_CONV), lay),
        _resident((LRU_CONV, D_LRU), lay), _resident((1, D_LRU), lay),
        _resident((D_LRU, D_LRU), lay), _resident((1, D_LRU), lay),
        _resident((D_LRU, D_LRU), lay), _resident((1, D_LRU), lay),
        _resident((1, D_LRU), lay),
        smem,
        _resident((1, D_MODEL), (layer, 1)), _resident((1, D_MODEL), (layer, 1)),
    ]
    scratch = [
        pltpu.VMEM((N_Q_HEADS, BLOCK, 2 * BLOCK), F32),
        pltpu.VMEM((CONV_HIST, D_CONV), F32),
        pltpu.VMEM((SUBLANES, CONV_HIST + T, D_CONV), F32),
        pltpu.VMEM((LRU_HIST + T, D_LRU), F32),
        pltpu.VMEM((1, D_LRU), F32),
    ] + [pltpu.VMEM((BLOCK + T, KV_DIM), BF16) for _ in range(6)] + [
        pltpu.VMEM((T, D_MODEL), BF16),
    ]
    return pl.pallas_call(
        functools.partial(_mixer_kernel, layer),
        grid=(B, S // T),
        in_specs=in_specs,
        out_specs=x_spec,
        out_shape=jax.ShapeDtypeStruct(x.shape, F32),
        scratch_shapes=scratch,
        compiler_params=pltpu.CompilerParams(
            dimension_semantics=("arbitrary", "arbitrary"), vmem_limit_bytes=VMEM_LIMIT_BYTES),
        name="mixer_ln",
    )(x, w_in, w_out, bucket, rel_bias, cw, cb, cg, cbeta, lw, lb, wa, ba, wx, bx, lam, sinks, ln_g, ln_b)


def _bucket_table():
    qi = np.arange(BLOCK)[:, None]
    kj = np.arange(2 * BLOCK)[None, :]
    dist = np.maximum(qi - kj + BLOCK, 0)
    max_exact = REL_BUCKETS // 2
    ratio = np.log(np.maximum(dist, 1).astype(np.float32) / np.float32(max_exact)) / np.float32(
        math.log(REL_MAX_DIST / max_exact))
    large = max_exact + (ratio * np.float32(REL_BUCKETS - max_exact)).astype(np.int32)
    return np.where(dist < max_exact, dist, np.minimum(large, REL_BUCKETS - 1)).astype(np.int32)


def _block_diag(w):
    l, h, d, _ = w.shape
    eye = jnp.eye(h, dtype=w.dtype)
    return (eye[None, :, None, :, None] * w[:, :, :, None, :]).reshape(l, h * d, h * d)


def kernel(x, rel_bias, ln_g, ln_b, ffn_w_gate, ffn_w_up, ffn_w_down, w_in, conv_dw_w, conv_dw_b, conv_ln_g, conv_ln_b, lru_conv_w, lru_conv_b, lru_wa, lru_ba, lru_wx, lru_bx, lru_lambda, attn_sinks, w_out):
    B, S, D = x.shape
    assert D == D_MODEL and S % SEQ_TILE == 0 and (B * S) % FFN_TILE == 0
    rows = lambda v: v.reshape(v.shape[:-1] + (1, v.shape[-1]))
    wg, wu, wd = ffn_w_gate.astype(BF16), ffn_w_up.astype(BF16), ffn_w_down.astype(BF16)
    w_in_b, w_out_b = w_in.astype(BF16), w_out.astype(BF16)
    wa, wx = _block_diag(lru_wa).astype(BF16), _block_diag(lru_wx).astype(BF16)
    g4, b4 = rows(ln_g), rows(ln_b)
    bucket = _bucket_table()
    for l in range(DEPTH):
        x = _ffn_call(x.reshape(B * S, D), wg, wu, wd, g4, b4, l, 0).reshape(B, S, D)
        x = _mixer_call(
            x, l, w_in_b, w_out_b, bucket, rel_bias,
            conv_dw_w, rows(conv_dw_b), rows(conv_ln_g), rows(conv_ln_b),
            lru_conv_w, rows(lru_conv_b), wa, rows(lru_ba), wx, rows(lru_bx), rows(lru_lambda),
            attn_sinks, g4, b4)
        x = _ffn_call(x.reshape(B * S, D), wg, wu, wd, g4, b4, l, 1).reshape(B, S, D)
    return x
```

```python
import functools
import math

import jax
import jax.numpy as jnp
import numpy as np
from jax import lax
from jax.experimental import pallas as pl
from jax.experimental.pallas import tpu as pltpu

D_MODEL = 1024
DEPTH = 2
D_CONV = 256
D_LRU = 256
D_ATTN = 512
CONV_KERNEL = 31
LRU_HEADS = 4
LRU_HEAD_DIM = 64
LRU_CONV = 4
LRU_C = 8.0
HEAD_DIM = 64
N_Q_HEADS = 8
N_KV_HEADS = 2
KV_DIM = 128
WINDOW = 128
BLOCK = 128
REL_BUCKETS = 32
REL_MAX_DIST = 128
D_FF = 2816
ALPHA = (2.0 * DEPTH) ** 0.25
LN_EPS = 1e-5
D_IN = 2 * D_CONV + 2 * D_LRU + D_ATTN + 2 * KV_DIM

LANES = 128
SUBLANES = 8
FF_CHUNK = 256
FFN_TILE = 512
SEQ_TILE = 512
ROW_BLOCK = 64
CONV_HIST = 32
LRU_HIST = 8
N_KV_BUFS = 6
VMEM_LIMIT_BYTES = 56 * 1024 * 1024

F32 = jnp.float32
BF16 = jnp.bfloat16


def _layernorm(z, g, b):
    mu = jnp.mean(z, axis=-1, keepdims=True)
    zc = z - mu
    var = jnp.mean(zc * zc, axis=-1, keepdims=True)
    return zc * lax.rsqrt(var + LN_EPS) * g + b


def _dot(a, b):
    return jnp.dot(a, b, preferred_element_type=F32)


def _dot_nt(a, b):
    return lax.dot_general(a, b, (((1,), (1,)), ((), ())), preferred_element_type=F32)


def _resident(shape, lead=()):
    block = (None,) * len(lead) + tuple(shape)
    index = tuple(lead) + (0,) * len(shape)
    return pl.BlockSpec(block, lambda *_: index, pipeline_mode=pl.Buffered(1))


def _ffn_kernel(x_ref, wg_ref, wu_ref, wd_ref, g_ref, b_ref, o_ref, h_ref):
    x = x_ref[...]
    xb = x.astype(BF16)
    for c in range(D_FF // FF_CHUNK):
        sl = slice(c * FF_CHUNK, (c + 1) * FF_CHUNK)
        gate = _dot(xb, wg_ref[:, sl])
        up = _dot(xb, wu_ref[:, sl])
        h_ref[:, sl] = (gate * jax.nn.sigmoid(gate) * up).astype(BF16)
    y = _dot(h_ref[...], wd_ref[...])
    o_ref[...] = _layernorm(ALPHA * x + 0.5 * y, g_ref[...], b_ref[...])


def _ffn_call(x2d, wg, wu, wd, ln_g, ln_b, layer, half):
    n_tok = x2d.shape[0]
    return pl.pallas_call(
        _ffn_kernel,
        grid=(n_tok // FFN_TILE,),
        in_specs=[
            pl.BlockSpec((FFN_TILE, D_MODEL), lambda i: (i, 0)),
            _resident((D_MODEL, D_FF), (layer, half)),
            _resident((D_MODEL, D_FF), (layer, half)),
            _resident((D_FF, D_MODEL), (layer, half)),
            _resident((1, D_MODEL), (layer, 2 * half)),
            _resident((1, D_MODEL), (layer, 2 * half)),
        ],
        out_specs=pl.BlockSpec((FFN_TILE, D_MODEL), lambda i: (i, 0)),
        out_shape=jax.ShapeDtypeStruct((n_tok, D_MODEL), F32),
        scratch_shapes=[pltpu.VMEM((FFN_TILE, D_FF), BF16)],
        compiler_params=pltpu.CompilerParams(
            dimension_semantics=("arbitrary",), vmem_limit_bytes=VMEM_LIMIT_BYTES),
        name="ffn_ln",
    )(x2d, wg, wu, wd, ln_g, ln_b)


def _mixer_kernel(layer,
                  x_ref, win_ref, wout_ref, bucket_ref, relb_ref,
                  cw_ref, cb_ref, cg_ref, cbeta_ref,
                  lw_ref, lb_ref, wa_ref, ba_ref, wx_ref, bx_ref, lam_ref,
                  sink_ref, g_ref, b_ref,
                  o_ref,
                  bias_tab, chist, xhist, hcar, u, y, cs, xb, *kv):
    T = SEQ_TILE
    s_idx = pl.program_id(1)

    @pl.when((pl.program_id(0) == 0) & (s_idx == 0))
    def _build_bias_table():
        bucket = bucket_ref[...]
        for head in range(N_Q_HEADS):
            tab = jnp.zeros((BLOCK, 2 * BLOCK), F32)
            for bkt in range(REL_BUCKETS):
                tab = jnp.where(bucket == bkt, relb_ref[bkt, head], tab)
            bias_tab[head] = tab

    @pl.when(s_idx == 0)
    def _start_of_sequence():
        chist[...] = jnp.zeros((CONV_HIST, D_CONV), F32)
        xhist[...] = jnp.zeros((LRU_HIST, D_LRU), F32)
        hcar[...] = jnp.zeros((1, D_LRU), F32)
        for buf in kv:
            buf[0:BLOCK, :] = jnp.zeros((BLOCK, KV_DIM), BF16)

    o1 = 2 * D_CONV
    o2 = o1 + 2 * D_LRU
    o3 = o2 + D_ATTN
    n_rb = T // ROW_BLOCK

    def in_proj():
        x_bf = x_ref[0].astype(BF16)
        for c0, c1 in ((0, o1), (o1, o2), (o3, D_IN), (o2, o3)):
            u[:, c0:c1] = _dot(x_bf, win_ref[:, c0:c1])

    def conv_prep():
        yglu = u[:, 0:D_CONV] * jax.nn.sigmoid(u[:, D_CONV:2 * D_CONV])
        n_rows = CONV_HIST + T
        whole = jnp.concatenate([chist[...], yglu], axis=0)
        cs[0] = whole
        for r in range(1, SUBLANES):
            cs[r] = pltpu.roll(whole, n_rows - r, 0)
        chist[...] = yglu[T - CONV_HIST:T, :]

    def conv_block(rb):
        off = CONV_HIST - (CONV_KERNEL - 1)
        acc = jnp.broadcast_to(cb_ref[...], (ROW_BLOCK, D_CONV))
        for tap in range(CONV_KERNEL):
            r = (off + tap) % SUBLANES
            first = rb * ROW_BLOCK + off + tap - r
            acc = acc + cw_ref[tap:tap + 1, :] * cs[r, first:first + ROW_BLOCK, :]
        yc = _layernorm(acc, cg_ref[...], cbeta_ref[...])
        y[rb * ROW_BLOCK:(rb + 1) * ROW_BLOCK, 0:D_CONV] = (yc * jax.nn.sigmoid(yc)).astype(BF16)

    lru = {}

    def lru_prep():
        lam = lam_ref[...]
        lru["log_sig"] = -(jnp.maximum(-lam, 0.0) + jnp.log1p(jnp.exp(-jnp.abs(lam))))
        x_new = u[:, o1:o1 + D_LRU]
        xb[0:LRU_HIST, :] = xhist[...]
        xb[LRU_HIST:LRU_HIST + T, :] = x_new
        xhist[...] = x_new[T - LRU_HIST:T, :]
        lru["carry"] = hcar[...]

    def lru_block(rb):
        off = LRU_HIST - (LRU_CONV - 1)
        halo = ROW_BLOCK + SUBLANES
        n_grp = ROW_BLOCK // SUBLANES
        sub = lax.broadcasted_iota(jnp.int32, (n_grp, SUBLANES, D_LRU), 1)
        rows = slice(rb * ROW_BLOCK, (rb + 1) * ROW_BLOCK)
        blk = xb[rb * ROW_BLOCK:rb * ROW_BLOCK + halo, :]
        xc = jnp.broadcast_to(lb_ref[...], (ROW_BLOCK, D_LRU))
        for tap in range(LRU_CONV):
            r = (off + tap) % SUBLANES
            first = off + tap - r
            src = blk if r == 0 else pltpu.roll(blk, halo - r, 0)
            xc = xc + lw_ref[tap:tap + 1, :] * src[first:first + ROW_BLOCK, :]
        xcb = xc.astype(BF16)
        r_gate = jax.nn.sigmoid(_dot(xcb, wa_ref[...]) + ba_ref[...])
        i_gate = jax.nn.sigmoid(_dot(xcb, wx_ref[...]) + bx_ref[...])
        log_a = LRU_C * r_gate * lru["log_sig"]
        a = jnp.exp(log_a)
        mult = jnp.sqrt(-jnp.tanh(log_a) * (a * a + 1.0))
        bterm = mult * (i_gate * xc)
        a3 = a.reshape(n_grp, SUBLANES, D_LRU)
        b3 = bterm.reshape(n_grp, SUBLANES, D_LRU)
        k = 1
        while k < SUBLANES:
            a_prev = jnp.where(sub >= k, pltpu.roll(a3, k, 1), 1.0)
            b_prev = jnp.where(sub >= k, pltpu.roll(b3, k, 1), 0.0)
            b3 = a3 * b_prev + b3
            a3 = a3 * a_prev
            k *= 2
        carry = lru["carry"]
        h_groups = []
        for grp_i in range(n_grp):
            h_grp = a3[grp_i] * carry + b3[grp_i]
            carry = h_grp[SUBLANES - 1:SUBLANES, :]
            h_groups.append(h_grp)
        lru["carry"] = carry
        h = jnp.concatenate(h_groups, axis=0)
        gb = u[rows, o1 + D_LRU:o1 + 2 * D_LRU]
        y[rows, D_CONV:D_CONV + D_LRU] = (h * jax.nn.gelu(gb)).astype(BF16)

    k_a, k_b, v0lo, v0hi, v1lo, v1hi = kv

    def kv_prep():
        kk = u[:, o3:o3 + KV_DIM]
        vv = u[:, o3 + KV_DIM:o3 + 2 * KV_DIM]
        lo = lax.broadcasted_iota(jnp.int32, (T, LANES), 1) < HEAD_DIM
        k_rot = pltpu.roll(kk, HEAD_DIM, 1)
        v_rot = pltpu.roll(vv, HEAD_DIM, 1)
        new_rows = slice(BLOCK, BLOCK + T)
        k_a[new_rows, :] = kk.astype(BF16)
        k_b[new_rows, :] = k_rot.astype(BF16)
        v0lo[new_rows, :] = jnp.where(lo, vv, 0.0).astype(BF16)
        v0hi[new_rows, :] = jnp.where(lo, 0.0, v_rot).astype(BF16)
        v1lo[new_rows, :] = jnp.where(lo, v_rot, 0.0).astype(BF16)
        v1hi[new_rows, :] = jnp.where(lo, 0.0, vv).astype(BF16)

    def attn(qb, c):
        qi = lax.broadcasted_iota(jnp.int32, (BLOCK, 2 * BLOCK), 0)
        kj = lax.broadcasted_iota(jnp.int32, (BLOCK, 2 * BLOCK), 1)
        dist = qi - kj + BLOCK
        valid = (dist >= 0) & (dist < WINDOW)
        if qb == 0:
            valid = valid & (kj >= jnp.where(s_idx == 0, BLOCK, 0))
        qlane_lo = lax.broadcasted_iota(jnp.int32, (BLOCK, LANES), 1) < HEAD_DIM
        r0 = qb * BLOCK
        q_rows = slice(r0, r0 + BLOCK)
        band = slice(r0, r0 + 2 * BLOCK)
        grp = c // (N_Q_HEADS // 2 // N_KV_HEADS)
        qc = u[q_rows, o2 + c * LANES:o2 + (c + 1) * LANES] * (HEAD_DIM ** -0.5)
        q_even = jnp.where(qlane_lo, qc, 0.0).astype(BF16)
        q_odd = jnp.where(qlane_lo, 0.0, qc).astype(BF16)
        k_even = (k_a if grp == 0 else k_b)[band, :]
        k_odd = (k_b if grp == 0 else k_a)[band, :]
        probs = []
        for hh, (qh, kh) in enumerate(((q_even, k_even), (q_odd, k_odd))):
            head = 2 * c + hh
            s = _dot_nt(qh, kh) + bias_tab[head]
            s = jnp.where(valid, s, -1e30)
            sink = sink_ref[layer, head]
            m = jnp.maximum(jnp.max(s, axis=-1, keepdims=True), sink)
            e = jnp.exp(s - m)
            den = jnp.sum(e, axis=-1, keepdims=True) + jnp.exp(sink - m)
            probs.append((e / den).astype(BF16))
        v_even = (v0lo if grp == 0 else v1lo)[band, :]
        v_odd = (v0hi if grp == 0 else v1hi)[band, :]
        o_pair = _dot(probs[0], v_even) + _dot(probs[1], v_odd)
        col = D_CONV + D_LRU + c * LANES
        y[q_rows, col:col + LANES] = o_pair.astype(BF16)

    def out_proj(qb):
        q_rows = slice(qb * BLOCK, (qb + 1) * BLOCK)
        hproj = _dot(y[q_rows, :], wout_ref[...])
        o_ref[0, q_rows, :] = _layernorm(ALPHA * x_ref[0, q_rows, :] + hproj, g_ref[...], b_ref[...])

    in_proj()
    conv_prep()
    for rb in range(n_rb):
        conv_block(rb)
    lru_prep()
    for rb in range(n_rb):
        lru_block(rb)
    kv_prep()
    for qb in range(T // BLOCK):
        for c in range(N_Q_HEADS // 2):
            attn(qb, c)
        out_proj(qb)
    hcar[...] = lru["carry"]
    for buf in kv:
        buf[0:BLOCK, :] = buf[T:T + BLOCK, :]


def _mixer_call(x, layer, w_in, w_out, bucket, rel_bias, cw, cb, cg, cbeta, lw, lb, wa, ba, wx, bx, lam,
                sinks, ln_g, ln_b):
    B, S, _ = x.shape
    T = SEQ_TILE
    x_spec = pl.BlockSpec((1, T, D_MODEL), lambda bi, si: (bi, si, 0))
    lay = (layer,)
    smem = pl.BlockSpec(memory_space=pltpu.SMEM)
    in_specs = [
        x_spec,
        _resident((D_MODEL, D_IN), lay),
        _resident((D_MODEL, D_MODEL), lay),
        _resident((BLOCK, 2 * BLOCK)),
        smem,
        _resident((CONV_KERNEL, D_CONV), lay), _resident((1, D_CONV), lay),
        _resident((1, D_CONV), lay), _resident((1, D_CONV), lay),
        _resident((LRU_CONV, D_LRU), lay), _resident((1, D_LRU), lay),
        _resident((D_LRU, D_LRU), lay), _resident((1, D_LRU), lay),
        _resident((D_LRU, D_LRU), lay), _resident((1, D_LRU), lay),
        _resident((1, D_LRU), lay),
        smem,
        _resident((1, D_MODEL), (layer, 1)), _resident((1, D_MODEL), (layer, 1)),
    ]
    scratch = (
        [pltpu.VMEM((N_Q_HEADS, BLOCK, 2 * BLOCK), F32),
         pltpu.VMEM((CONV_HIST, D_CONV), F32),
         pltpu.VMEM((LRU_HIST, D_LRU), F32),
         pltpu.VMEM((1, D_LRU), F32),
         pltpu.VMEM((T, D_IN), F32),
         pltpu.VMEM((T, D_MODEL), BF16),
         pltpu.VMEM((SUBLANES, CONV_HIST + T, D_CONV), F32),
         pltpu.VMEM((LRU_HIST + T, D_LRU), F32)]
        + [pltpu.VMEM((BLOCK + T, KV_DIM), BF16) for _ in range(N_KV_BUFS)])
    return pl.pallas_call(
        functools.partial(_mixer_kernel, layer),
        grid=(B, S // T),
        in_specs=in_specs,
        out_specs=x_spec,
        out_shape=jax.ShapeDtypeStruct(x.shape, F32),
        scratch_shapes=scratch,
        compiler_params=pltpu.CompilerParams(
            dimension_semantics=("arbitrary", "arbitrary"), vmem_limit_bytes=VMEM_LIMIT_BYTES),
        name="mixer_ln",
    )(x, w_in, w_out, bucket, rel_bias, cw, cb, cg, cbeta, lw, lb, wa, ba, wx, bx, lam, sinks, ln_g, ln_b)


def _bucket_table():
    qi = np.arange(BLOCK)[:, None]
    kj = np.arange(2 * BLOCK)[None, :]
    dist = np.maximum(qi - kj + BLOCK, 0)
    max_exact = REL_BUCKETS // 2
    ratio = np.log(np.maximum(dist, 1).astype(np.float32) / np.float32(max_exact)) / np.float32(
        math.log(REL_MAX_DIST / max_exact))
    large = max_exact + (ratio * np.float32(REL_BUCKETS - max_exact)).astype(np.int32)
    return np.where(dist < max_exact, dist, np.minimum(large, REL_BUCKETS - 1)).astype(np.int32)


def _block_diag(w):
    l, h, d, _ = w.shape
    eye = jnp.eye(h, dtype=w.dtype)
    return (eye[None, :, None, :, None] * w[:, :, :, None, :]).reshape(l, h * d, h * d)


def kernel(x, rel_bias, ln_g, ln_b, ffn_w_gate, ffn_w_up, ffn_w_down, w_in, conv_dw_w, conv_dw_b, conv_ln_g, conv_ln_b, lru_conv_w, lru_conv_b, lru_wa, lru_ba, lru_wx, lru_bx, lru_lambda, attn_sinks, w_out):
    B, S, D = x.shape
    assert D == D_MODEL and S % SEQ_TILE == 0 and (B * S) % FFN_TILE == 0
    rows = lambda v: v.reshape(v.shape[:-1] + (1, v.shape[-1]))
    wg, wu, wd = ffn_w_gate.astype(BF16), ffn_w_up.astype(BF16), ffn_w_down.astype(BF16)
    w_in_b, w_out_b = w_in.astype(BF16), w_out.astype(BF16)
    wa, wx = _block_diag(lru_wa).astype(BF16), _block_diag(lru_wx).astype(BF16)
    g4, b4 = rows(ln_g), rows(ln_b)
    bucket = _bucket_table()
    for l in range(DEPTH):
        x = _ffn_call(x.reshape(B * S, D), wg, wu, wd, g4, b4, l, 0).reshape(B, S, D)
        x = _mixer_call(
            x, l, w_in_b, w_out_b, bucket, rel_bias,
            conv_dw_w, rows(conv_dw_b), rows(conv_ln_g), rows(conv_ln_b),
            lru_conv_w, rows(lru_conv_b), wa, rows(lru_ba), wx, rows(lru_bx), rows(lru_lambda),
            attn_sinks, g4, b4)
        x = _ffn_call(x.reshape(B * S, D), wg, wu, wd, g4, b4, l, 1).reshape(B, S, D)
    return x
```

```python
import functools
import math

import jax
import jax.numpy as jnp
import numpy as np
from jax import lax
from jax.experimental import pallas as pl
from jax.experimental.pallas import tpu as pltpu

D_MODEL = 1024
DEPTH = 2
D_CONV = 256
D_LRU = 256
D_ATTN = 512
CONV_KERNEL = 31
LRU_HEADS = 4
LRU_HEAD_DIM = 64
LRU_CONV = 4
LRU_C = 8.0
HEAD_DIM = 64
N_Q_HEADS = 8
N_KV_HEADS = 2
KV_DIM = 128
WINDOW = 128
BLOCK = 128
REL_BUCKETS = 32
REL_MAX_DIST = 128
D_FF = 2816
ALPHA = (2.0 * DEPTH) ** 0.25
LN_EPS = 1e-5
D_IN = 2 * D_CONV + 2 * D_LRU + D_ATTN + 2 * KV_DIM

LANES = 128
SUBLANES = 8
FF_CHUNK = 256
FFN_TILE = 512
SEQ_TILE = 512
ROW_BLOCK = 64
CONV_HIST = 32
LRU_HIST = 8
N_KV_BUFS = 6
VMEM_LIMIT_BYTES = 56 * 1024 * 1024

F32 = jnp.float32
BF16 = jnp.bfloat16


def _layernorm(z, g, b):
    mu = jnp.mean(z, axis=-1, keepdims=True)
    zc = z - mu
    var = jnp.mean(zc * zc, axis=-1, keepdims=True)
    return zc * lax.rsqrt(var + LN_EPS) * g + b


def _dot(a, b):
    return jnp.dot(a, b, preferred_element_type=F32)


def _dot_nt(a, b):
    return lax.dot_general(a, b, (((1,), (1,)), ((), ())), preferred_element_type=F32)


def _resident(shape, lead=()):
    block = (None,) * len(lead) + tuple(shape)
    index = tuple(lead) + (0,) * len(shape)
    return pl.BlockSpec(block, lambda *_: index, pipeline_mode=pl.Buffered(1))


def _ffn_kernel(n_tiles, x_ref, wg_ref, wu_ref, wd_ref, g_ref, b_ref, o_ref, h_ref, z_ref):
    i = pl.program_id(0)

    def normalise_previous():
        out = _layernorm(z_ref[...], g_ref[...], b_ref[...])
        o_ref[...] = out
        return out

    @pl.when(i == 0)
    def _():
        z_ref[...] = jnp.zeros((FFN_TILE, D_MODEL), F32)

    @pl.when(i < n_tiles)
    def _():
        out_prev = normalise_previous()
        bits = pltpu.bitcast(out_prev, jnp.uint32)
        acc = bits[0:SUBLANES, :]
        for grp in range(1, FFN_TILE // SUBLANES):
            acc = acc | bits[grp * SUBLANES:(grp + 1) * SUBLANES, :]
        word = acc[:, 0:LANES]
        for blk in range(1, D_MODEL // LANES):
            word = word | acc[:, blk * LANES:(blk + 1) * LANES]
        zero = pltpu.bitcast(lax.shift_right_logical(lax.shift_right_logical(word, jnp.uint32(16)), jnp.uint32(16)), F32)
        zero_rows = pltpu.repeat(pltpu.repeat(zero, FFN_TILE // SUBLANES, axis=0), FF_CHUNK // LANES, axis=1)
        x = x_ref[...]
        xb = x.astype(BF16)
        n_chunks = D_FF // FF_CHUNK
        for c in range(n_chunks):
            sl = slice(c * FF_CHUNK, (c + 1) * FF_CHUNK)
            gate = _dot(xb, wg_ref[:, sl])
            up = _dot(xb, wu_ref[:, sl])
            if c == n_chunks - 1:
                up = up + zero_rows
            h_ref[:, sl] = (gate * jax.nn.sigmoid(gate) * up).astype(BF16)
        y = _dot(h_ref[...], wd_ref[...])
        z_ref[...] = ALPHA * x + 0.5 * y

    @pl.when(i == n_tiles)
    def _():
        normalise_previous()


def _ffn_call(x2d, wg, wu, wd, ln_g, ln_b, layer, half):
    n_tok = x2d.shape[0]
    n_tiles = n_tok // FFN_TILE
    return pl.pallas_call(
        functools.partial(_ffn_kernel, n_tiles),
        grid=(n_tiles + 1,),
        in_specs=[
            pl.BlockSpec((FFN_TILE, D_MODEL), lambda i: (jnp.minimum(i, n_tiles - 1), 0)),
            _resident((D_MODEL, D_FF), (layer, half)),
            _resident((D_MODEL, D_FF), (layer, half)),
            _resident((D_FF, D_MODEL), (layer, half)),
            _resident((1, D_MODEL), (layer, 2 * half)),
            _resident((1, D_MODEL), (layer, 2 * half)),
        ],
        out_specs=pl.BlockSpec((FFN_TILE, D_MODEL), lambda i: (jnp.maximum(i - 1, 0), 0)),
        out_shape=jax.ShapeDtypeStruct((n_tok, D_MODEL), F32),
        scratch_shapes=[pltpu.VMEM((FFN_TILE, D_FF), BF16), pltpu.VMEM((FFN_TILE, D_MODEL), F32)],
        compiler_params=pltpu.CompilerParams(
            dimension_semantics=("arbitrary",), vmem_limit_bytes=VMEM_LIMIT_BYTES),
        name="ffn_ln",
    )(x2d, wg, wu, wd, ln_g, ln_b)


def _mixer_kernel(layer,
                  x_ref, win_ref, wout_ref, bucket_ref, relb_ref,
                  cw_ref, cb_ref, cg_ref, cbeta_ref,
                  lw_ref, lb_ref, wa_ref, ba_ref, wx_ref, bx_ref, lam_ref,
                  sink_ref, g_ref, b_ref,
                  o_ref,
                  bias_tab, chist, xhist, hcar, u, y, cs, xb, *kv):
    T = SEQ_TILE
    s_idx = pl.program_id(1)

    @pl.when((pl.program_id(0) == 0) & (s_idx == 0))
    def _build_bias_table():
        bucket = bucket_ref[...]
        for head in range(N_Q_HEADS):
            tab = jnp.zeros((BLOCK, 2 * BLOCK), F32)
            for bkt in range(REL_BUCKETS):
                tab = jnp.where(bucket == bkt, relb_ref[bkt, head], tab)
            bias_tab[head] = tab

    @pl.when(s_idx == 0)
    def _start_of_sequence():
        chist[...] = jnp.zeros((CONV_HIST, D_CONV), F32)
        xhist[...] = jnp.zeros((LRU_HIST, D_LRU), F32)
        hcar[...] = jnp.zeros((1, D_LRU), F32)
        for buf in kv:
            buf[0:BLOCK, :] = jnp.zeros((BLOCK, KV_DIM), BF16)

    o1 = 2 * D_CONV
    o2 = o1 + 2 * D_LRU
    o3 = o2 + D_ATTN
    n_rb = T // ROW_BLOCK

    def in_proj():
        x_bf = x_ref[0].astype(BF16)
        for c0, c1 in ((0, o1), (o1, o2), (o3, D_IN), (o2, o3)):
            u[:, c0:c1] = _dot(x_bf, win_ref[:, c0:c1])

    def conv_prep():
        yglu = u[:, 0:D_CONV] * jax.nn.sigmoid(u[:, D_CONV:2 * D_CONV])
        n_rows = CONV_HIST + T
        whole = jnp.concatenate([chist[...], yglu], axis=0)
        cs[0] = whole
        for r in range(1, SUBLANES):
            cs[r] = pltpu.roll(whole, n_rows - r, 0)
        chist[...] = yglu[T - CONV_HIST:T, :]

    def conv_block(rb):
        off = CONV_HIST - (CONV_KERNEL - 1)
        acc = jnp.broadcast_to(cb_ref[...], (ROW_BLOCK, D_CONV))
        for tap in range(CONV_KERNEL):
            r = (off + tap) % SUBLANES
            first = rb * ROW_BLOCK + off + tap - r
            acc = acc + cw_ref[tap:tap + 1, :] * cs[r, first:first + ROW_BLOCK, :]
        yc = _layernorm(acc, cg_ref[...], cbeta_ref[...])
        y[rb * ROW_BLOCK:(rb + 1) * ROW_BLOCK, 0:D_CONV] = (yc * jax.nn.sigmoid(yc)).astype(BF16)

    lru = {}

    def lru_prep():
        lam = lam_ref[...]
        lru["log_sig"] = -(jnp.maximum(-lam, 0.0) + jnp.log1p(jnp.exp(-jnp.abs(lam))))
        x_new = u[:, o1:o1 + D_LRU]
        xb[0:LRU_HIST, :] = xhist[...]
        xb[LRU_HIST:LRU_HIST + T, :] = x_new
        xhist[...] = x_new[T - LRU_HIST:T, :]
        lru["carry"] = hcar[...]

    def lru_block(rb):
        off = LRU_HIST - (LRU_CONV - 1)
        halo = ROW_BLOCK + SUBLANES
        n_grp = ROW_BLOCK // SUBLANES
        sub = lax.broadcasted_iota(jnp.int32, (n_grp, SUBLANES, D_LRU), 1)
        rows = slice(rb * ROW_BLOCK, (rb + 1) * ROW_BLOCK)
        blk = xb[rb * ROW_BLOCK:rb * ROW_BLOCK + halo, :]
        xc = jnp.broadcast_to(lb_ref[...], (ROW_BLOCK, D_LRU))
        for tap in range(LRU_CONV):
            r = (off + tap) % SUBLANES
            first = off + tap - r
            src = blk if r == 0 else pltpu.roll(blk, halo - r, 0)
            xc = xc + lw_ref[tap:tap + 1, :] * src[first:first + ROW_BLOCK, :]
        xcb = xc.astype(BF16)
        r_gate = jax.nn.sigmoid(_dot(xcb, wa_ref[...]) + ba_ref[...])
        i_gate = jax.nn.sigmoid(_dot(xcb, wx_ref[...]) + bx_ref[...])
        log_a = LRU_C * r_gate * lru["log_sig"]
        a = jnp.exp(log_a)
        mult = jnp.sqrt(-jnp.tanh(log_a) * (a * a + 1.0))
        bterm = mult * (i_gate * xc)
        a3 = a.reshape(n_grp, SUBLANES, D_LRU)
        b3 = bterm.reshape(n_grp, SUBLANES, D_LRU)
        k = 1
        while k < SUBLANES:
            a_prev = jnp.where(sub >= k, pltpu.roll(a3, k, 1), 1.0)
            b_prev = jnp.where(sub >= k, pltpu.roll(b3, k, 1), 0.0)
            b3 = a3 * b_prev + b3
            a3 = a3 * a_prev
            k *= 2
        carry = lru["carry"]
        h_groups = []
        for grp_i in range(n_grp):
            h_grp = a3[grp_i] * carry + b3[grp_i]
            carry = h_grp[SUBLANES - 1:SUBLANES, :]
            h_groups.append(h_grp)
        lru["carry"] = carry
        h = jnp.concatenate(h_groups, axis=0)
        gb = u[rows, o1 + D_LRU:o1 + 2 * D_LRU]
        y[rows, D_CONV:D_CONV + D_LRU] = (h * jax.nn.gelu(gb)).astype(BF16)

    k_a, k_b, v0lo, v0hi, v1lo, v1hi = kv

    def kv_prep():
        kk = u[:, o3:o3 + KV_DIM]
        vv = u[:, o3 + KV_DIM:o3 + 2 * KV_DIM]
        lo = lax.broadcasted_iota(jnp.int32, (T, LANES), 1) < HEAD_DIM
        k_rot = pltpu.roll(kk, HEAD_DIM, 1)
        v_rot = pltpu.roll(vv, HEAD_DIM, 1)
        new_rows = slice(BLOCK, BLOCK + T)
        k_a[new_rows, :] = kk.astype(BF16)
        k_b[new_rows, :] = k_rot.astype(BF16)
        v0lo[new_rows, :] = jnp.where(lo, vv, 0.0).astype(BF16)
        v0hi[new_rows, :] = jnp.where(lo, 0.0, v_rot).astype(BF16)
        v1lo[new_rows, :] = jnp.where(lo, v_rot, 0.0).astype(BF16)
        v1hi[new_rows, :] = jnp.where(lo, 0.0, vv).astype(BF16)

    def attn(qb, c):
        qi = lax.broadcasted_iota(jnp.int32, (BLOCK, 2 * BLOCK), 0)
        kj = lax.broadcasted_iota(jnp.int32, (BLOCK, 2 * BLOCK), 1)
        dist = qi - kj + BLOCK
        valid = (dist >= 0) & (dist < WINDOW)
        if qb == 0:
            valid = valid & (kj >= jnp.where(s_idx == 0, BLOCK, 0))
        qlane_lo = lax.broadcasted_iota(jnp.int32, (BLOCK, LANES), 1) < HEAD_DIM
        r0 = qb * BLOCK
        q_rows = slice(r0, r0 + BLOCK)
        band = slice(r0, r0 + 2 * BLOCK)
        grp = c // (N_Q_HEADS // 2 // N_KV_HEADS)
        qc = u[q_rows, o2 + c * LANES:o2 + (c + 1) * LANES] * (HEAD_DIM ** -0.5)
        q_even = jnp.where(qlane_lo, qc, 0.0).astype(BF16)
        q_odd = jnp.where(qlane_lo, 0.0, qc).astype(BF16)
        k_even = (k_a if grp == 0 else k_b)[band, :]
        k_odd = (k_b if grp == 0 else k_a)[band, :]
        probs = []
        for hh, (qh, kh) in enumerate(((q_even, k_even), (q_odd, k_odd))):
            head = 2 * c + hh
            s = _dot_nt(qh, kh) + bias_tab[head]
            s = jnp.where(valid, s, -1e30)
            sink = sink_ref[layer, head]
            m = jnp.maximum(jnp.max(s, axis=-1, keepdims=True), sink)
            e = jnp.exp(s - m)
            den = jnp.sum(e, axis=-1, keepdims=True) + jnp.exp(sink - m)
            probs.append((e / den).astype(BF16))
        v_even = (v0lo if grp == 0 else v1lo)[band, :]
        v_odd = (v0hi if grp == 0 else v1hi)[band, :]
        o_pair = _dot(probs[0], v_even) + _dot(probs[1], v_odd)
        col = D_CONV + D_LRU + c * LANES
        y[q_rows, col:col + LANES] = o_pair.astype(BF16)

    def out_proj(qb):
        q_rows = slice(qb * BLOCK, (qb + 1) * BLOCK)
        hproj = _dot(y[q_rows, :], wout_ref[...])
        o_ref[0, q_rows, :] = _layernorm(ALPHA * x_ref[0, q_rows, :] + hproj, g_ref[...], b_ref[...])

    in_proj()
    conv_prep()
    for rb in range(n_rb):
        conv_block(rb)
    lru_prep()
    for rb in range(n_rb):
        lru_block(rb)
    kv_prep()
    for qb in range(T // BLOCK):
        for c in range(N_Q_HEADS // 2):
            attn(qb, c)
        out_proj(qb)
    hcar[...] = lru["carry"]
    for buf in kv:
        buf[0:BLOCK, :] = buf[T:T + BLOCK, :]


def _mixer_call(x, layer, w_in, w_out, bucket, rel_bias, cw, cb, cg, cbeta, lw, lb, wa, ba, wx, bx, lam,
                sinks, ln_g, ln_b):
    B, S, _ = x.shape
    T = SEQ_TILE
    x_spec = pl.BlockSpec((1, T, D_MODEL), lambda bi, si: (bi, si, 0))
    lay = (layer,)
    smem = pl.BlockSpec(memory_space=pltpu.SMEM)
    in_specs = [
        x_spec,
        _resident((D_MODEL, D_IN), lay),
        _resident((D_MODEL, D_MODEL), lay),
        _resident((BLOCK, 2 * BLOCK)),
        smem,
        _resident((CONV_KERNEL, D_CONV), lay), _resident((1, D_CONV), lay),
        _resident((1, D_CONV), lay), _resident((1, D_CONV), lay),
        _resident((LRU_CONV, D_LRU), lay), _resident((1, D_LRU), lay),
        _resident((D_LRU, D_LRU), lay), _resident((1, D_LRU), lay),
        _resident((D_LRU, D_LRU), lay), _resident((1, D_LRU), lay),
        _resident((1, D_LRU), lay),
        smem,
        _resident((1, D_MODEL), (layer, 1)), _resident((1, D_MODEL), (layer, 1)),
    ]
    scratch = (
        [pltpu.VMEM((N_Q_HEADS, BLOCK, 2 * BLOCK), F32),
         pltpu.VMEM((CONV_HIST, D_CONV), F32),
         pltpu.VMEM((LRU_HIST, D_LRU), F32),
         pltpu.VMEM((1, D_LRU), F32),
         pltpu.VMEM((T, D_IN), F32),
         pltpu.VMEM((T, D_MODEL), BF16),
         pltpu.VMEM((SUBLANES, CONV_HIST + T, D_CONV), F32),
         pltpu.VMEM((LRU_HIST + T, D_LRU), F32)]
        + [pltpu.VMEM((BLOCK + T, KV_DIM), BF16) for _ in range(N_KV_BUFS)])
    return pl.pallas_call(
        functools.partial(_mixer_kernel, layer),
        grid=(B, S // T),
        in_specs=in_specs,
        out_specs=x_spec,
        out_shape=jax.ShapeDtypeStruct(x.shape, F32),
        scratch_shapes=scratch,
        compiler_params=pltpu.CompilerParams(
            dimension_semantics=("arbitrary", "arbitrary"), vmem_limit_bytes=VMEM_LIMIT_BYTES),
        name="mixer_ln",
    )(x, w_in, w_out, bucket, rel_bias, cw, cb, cg, cbeta, lw, lb, wa, ba, wx, bx, lam, sinks, ln_g, ln_b)


def _bucket_table():
    qi = np.arange(BLOCK)[:, None]
    kj = np.arange(2 * BLOCK)[None, :]
    dist = np.maximum(qi - kj + BLOCK, 0)
    max_exact = REL_BUCKETS // 2
    ratio = np.log(np.maximum(dist, 1).astype(np.float32) / np.float32(max_exact)) / np.float32(
        math.log(REL_MAX_DIST / max_exact))
    large = max_exact + (ratio * np.float32(REL_BUCKETS - max_exact)).astype(np.int32)
    return np.where(dist < max_exact, dist, np.minimum(large, REL_BUCKETS - 1)).astype(np.int32)


def _block_diag(w):
    l, h, d, _ = w.shape
    eye = jnp.eye(h, dtype=w.dtype)
    return (eye[None, :, None, :, None] * w[:, :, :, None, :]).reshape(l, h * d, h * d)


def kernel(x, rel_bias, ln_g, ln_b, ffn_w_gate, ffn_w_up, ffn_w_down, w_in, conv_dw_w, conv_dw_b, conv_ln_g, conv_ln_b, lru_conv_w, lru_conv_b, lru_wa, lru_ba, lru_wx, lru_bx, lru_lambda, attn_sinks, w_out):
    B, S, D = x.shape
    assert D == D_MODEL and S % SEQ_TILE == 0 and (B * S) % FFN_TILE == 0
    rows = lambda v: v.reshape(v.shape[:-1] + (1, v.shape[-1]))
    wg, wu, wd = ffn_w_gate.astype(BF16), ffn_w_up.astype(BF16), ffn_w_down.astype(BF16)
    w_in_b, w_out_b = w_in.astype(BF16), w_out.astype(BF16)
    wa, wx = _block_diag(lru_wa).astype(BF16), _block_diag(lru_wx).astype(BF16)
    g4, b4 = rows(ln_g), rows(ln_b)
    bucket = _bucket_table()
    for l in range(DEPTH):
        x = _ffn_call(x.reshape(B * S, D), wg, wu, wd, g4, b4, l, 0).reshape(B, S, D)
        x = _mixer_call(
            x, l, w_in_b, w_out_b, bucket, rel_bias,
            conv_dw_w, rows(conv_dw_b), rows(conv_ln_g), rows(conv_ln_b),
            lru_conv_w, rows(lru_conv_b), wa, rows(lru_ba), wx, rows(lru_bx), rows(lru_lambda),
            attn_sinks, g4, b4)
        x = _ffn_call(x.reshape(B * S, D), wg, wu, wd, g4, b4, l, 1).reshape(B, S, D)
    return x
```

```python
import functools
import math

import jax
import jax.numpy as jnp
import numpy as np
from jax import lax
from jax.experimental import pallas as pl
from jax.experimental.pallas import tpu as pltpu

D_MODEL = 1024
DEPTH = 2
D_CONV = 256
D_LRU = 256
D_ATTN = 512
CONV_KERNEL = 31
LRU_HEADS = 4
LRU_HEAD_DIM = 64
LRU_CONV = 4
LRU_C = 8.0
HEAD_DIM = 64
N_Q_HEADS = 8
N_KV_HEADS = 2
KV_DIM = 128
WINDOW = 128
BLOCK = 128
REL_BUCKETS = 32
REL_MAX_DIST = 128
D_FF = 2816
ALPHA = (2.0 * DEPTH) ** 0.25
LN_EPS = 1e-5
D_IN = 2 * D_CONV + 2 * D_LRU + D_ATTN + 2 * KV_DIM

LANES = 128
SUBLANES = 8
FF_CHUNK = 256
FFN_TILE = 512
SEQ_TILE = 512
ROW_BLOCK = 64
CONV_HIST = 32
LRU_HIST = 8
N_KV_BUFS = 6
VMEM_LIMIT_BYTES = 56 * 1024 * 1024

F32 = jnp.float32
BF16 = jnp.bfloat16


def _layernorm(z, g, b):
    mu = jnp.mean(z, axis=-1, keepdims=True)
    zc = z - mu
    var = jnp.mean(zc * zc, axis=-1, keepdims=True)
    return zc * lax.rsqrt(var + LN_EPS) * g + b


def _dot(a, b):
    return jnp.dot(a, b, preferred_element_type=F32)


def _dot_nt(a, b):
    return lax.dot_general(a, b, (((1,), (1,)), ((), ())), preferred_element_type=F32)


def _resident(shape, lead=()):
    block = (None,) * len(lead) + tuple(shape)
    index = tuple(lead) + (0,) * len(shape)
    return pl.BlockSpec(block, lambda *_: index, pipeline_mode=pl.Buffered(1))


def _ffn_kernel(n_tiles, x_ref, wg_ref, wu_ref, wd_ref, g_ref, b_ref, o_ref, h_ref, z_ref):
    i = pl.program_id(0)

    def normalise_previous():
        out = _layernorm(z_ref[...], g_ref[...], b_ref[...])
        o_ref[...] = out
        return out

    @pl.when(i == 0)
    def _():
        z_ref[...] = jnp.zeros((FFN_TILE, D_MODEL), F32)

    @pl.when(i < n_tiles)
    def _():
        out_prev = normalise_previous()
        bits = pltpu.bitcast(out_prev, jnp.uint32)
        acc = bits[0:SUBLANES, :]
        for grp in range(1, FFN_TILE // SUBLANES):
            acc = acc | bits[grp * SUBLANES:(grp + 1) * SUBLANES, :]
        word = acc[:, 0:LANES]
        for blk in range(1, D_MODEL // LANES):
            word = word | acc[:, blk * LANES:(blk + 1) * LANES]
        zero = pltpu.bitcast(lax.shift_right_logical(lax.shift_right_logical(word, jnp.uint32(16)), jnp.uint32(16)), F32)
        zero_rows = jnp.broadcast_to(zero[0:1, 0:1], (FFN_TILE, FF_CHUNK))
        x = x_ref[...]
        xb = x.astype(BF16)
        n_chunks = D_FF // FF_CHUNK
        for c in range(n_chunks):
            sl = slice(c * FF_CHUNK, (c + 1) * FF_CHUNK)
            gate = _dot(xb, wg_ref[:, sl])
            up = _dot(xb, wu_ref[:, sl])
            if c == n_chunks - 1:
                up = up + zero_rows
            h_ref[:, sl] = (gate * jax.nn.sigmoid(gate) * up).astype(BF16)
        y = _dot(h_ref[...], wd_ref[...])
        z_ref[...] = ALPHA * x + 0.5 * y

    @pl.when(i == n_tiles)
    def _():
        normalise_previous()


def _ffn_call(x2d, wg, wu, wd, ln_g, ln_b, layer, half):
    n_tok = x2d.shape[0]
    n_tiles = n_tok // FFN_TILE
    return pl.pallas_call(
        functools.partial(_ffn_kernel, n_tiles),
        grid=(n_tiles + 1,),
        in_specs=[
            pl.BlockSpec((FFN_TILE, D_MODEL), lambda i: (jnp.minimum(i, n_tiles - 1), 0)),
            _resident((D_MODEL, D_FF), (layer, half)),
            _resident((D_MODEL, D_FF), (layer, half)),
            _resident((D_FF, D_MODEL), (layer, half)),
            _resident((1, D_MODEL), (layer, 2 * half)),
            _resident((1, D_MODEL), (layer, 2 * half)),
        ],
        out_specs=pl.BlockSpec((FFN_TILE, D_MODEL), lambda i: (jnp.maximum(i - 1, 0), 0)),
        out_shape=jax.ShapeDtypeStruct((n_tok, D_MODEL), F32),
        scratch_shapes=[pltpu.VMEM((FFN_TILE, D_FF), BF16), pltpu.VMEM((FFN_TILE, D_MODEL), F32)],
        compiler_params=pltpu.CompilerParams(
            dimension_semantics=("arbitrary",), vmem_limit_bytes=VMEM_LIMIT_BYTES),
        name="ffn_ln",
    )(x2d, wg, wu, wd, ln_g, ln_b)


def _mixer_kernel(layer,
                  x_ref, win_ref, wout_ref, bucket_ref, relb_ref,
                  cw_ref, cb_ref, cg_ref, cbeta_ref,
                  lw_ref, lb_ref, wa_ref, ba_ref, wx_ref, bx_ref, lam_ref,
                  sink_ref, g_ref, b_ref,
                  o_ref,
                  bias_tab, chist, xhist, hcar, u, y, cs, xb, *kv):
    T = SEQ_TILE
    s_idx = pl.program_id(1)

    @pl.when((pl.program_id(0) == 0) & (s_idx == 0))
    def _build_bias_table():
        bucket = bucket_ref[...]
        for head in range(N_Q_HEADS):
            tab = jnp.zeros((BLOCK, 2 * BLOCK), F32)
            for bkt in range(REL_BUCKETS):
                tab = jnp.where(bucket == bkt, relb_ref[bkt, head], tab)
            bias_tab[head] = tab

    @pl.when(s_idx == 0)
    def _start_of_sequence():
        chist[...] = jnp.zeros((CONV_HIST, D_CONV), F32)
        xhist[...] = jnp.zeros((LRU_HIST, D_LRU), F32)
        hcar[...] = jnp.zeros((1, D_LRU), F32)
        for buf in kv:
            buf[0:BLOCK, :] = jnp.zeros((BLOCK, KV_DIM), BF16)

    o1 = 2 * D_CONV
    o2 = o1 + 2 * D_LRU
    o3 = o2 + D_ATTN
    n_rb = T // ROW_BLOCK

    def in_proj():
        x_bf = x_ref[0].astype(BF16)
        for c0, c1 in ((0, o1), (o1, o2), (o3, D_IN), (o2, o3)):
            u[:, c0:c1] = _dot(x_bf, win_ref[:, c0:c1])

    def conv_prep():
        yglu = u[:, 0:D_CONV] * jax.nn.sigmoid(u[:, D_CONV:2 * D_CONV])
        n_rows = CONV_HIST + T
        whole = jnp.concatenate([chist[...], yglu], axis=0)
        cs[0] = whole
        for r in range(1, SUBLANES):
            cs[r] = pltpu.roll(whole, n_rows - r, 0)
        chist[...] = yglu[T - CONV_HIST:T, :]

    def conv_block(rb):
        off = CONV_HIST - (CONV_KERNEL - 1)
        acc = jnp.broadcast_to(cb_ref[...], (ROW_BLOCK, D_CONV))
        for tap in range(CONV_KERNEL):
            r = (off + tap) % SUBLANES
            first = rb * ROW_BLOCK + off + tap - r
            acc = acc + cw_ref[tap:tap + 1, :] * cs[r, first:first + ROW_BLOCK, :]
        yc = _layernorm(acc, cg_ref[...], cbeta_ref[...])
        y[rb * ROW_BLOCK:(rb + 1) * ROW_BLOCK, 0:D_CONV] = (yc * jax.nn.sigmoid(yc)).astype(BF16)

    lru = {}

    def lru_prep():
        lam = lam_ref[...]
        lru["log_sig"] = -(jnp.maximum(-lam, 0.0) + jnp.log1p(jnp.exp(-jnp.abs(lam))))
        x_new = u[:, o1:o1 + D_LRU]
        xb[0:LRU_HIST, :] = xhist[...]
        xb[LRU_HIST:LRU_HIST + T, :] = x_new
        xhist[...] = x_new[T - LRU_HIST:T, :]
        lru["carry"] = hcar[...]

    def lru_block(rb):
        off = LRU_HIST - (LRU_CONV - 1)
        halo = ROW_BLOCK + SUBLANES
        n_grp = ROW_BLOCK // SUBLANES
        sub = lax.broadcasted_iota(jnp.int32, (n_grp, SUBLANES, D_LRU), 1)
        rows = slice(rb * ROW_BLOCK, (rb + 1) * ROW_BLOCK)
        blk = xb[rb * ROW_BLOCK:rb * ROW_BLOCK + halo, :]
        xc = jnp.broadcast_to(lb_ref[...], (ROW_BLOCK, D_LRU))
        for tap in range(LRU_CONV):
            r = (off + tap) % SUBLANES
            first = off + tap - r
            src = blk if r == 0 else pltpu.roll(blk, halo - r, 0)
            xc = xc + lw_ref[tap:tap + 1, :] * src[first:first + ROW_BLOCK, :]
        xcb = xc.astype(BF16)
        r_gate = jax.nn.sigmoid(_dot(xcb, wa_ref[...]) + ba_ref[...])
        i_gate = jax.nn.sigmoid(_dot(xcb, wx_ref[...]) + bx_ref[...])
        log_a = LRU_C * r_gate * lru["log_sig"]
        a = jnp.exp(log_a)
        mult = jnp.sqrt(-jnp.tanh(log_a) * (a * a + 1.0))
        bterm = mult * (i_gate * xc)
        a3 = a.reshape(n_grp, SUBLANES, D_LRU)
        b3 = bterm.reshape(n_grp, SUBLANES, D_LRU)
        k = 1
        while k < SUBLANES:
            a_prev = jnp.where(sub >= k, pltpu.roll(a3, k, 1), 1.0)
            b_prev = jnp.where(sub >= k, pltpu.roll(b3, k, 1), 0.0)
            b3 = a3 * b_prev + b3
            a3 = a3 * a_prev
            k *= 2
        carry = lru["carry"]
        h_groups = []
        for grp_i in range(n_grp):
            h_grp = a3[grp_i] * carry + b3[grp_i]
            carry = h_grp[SUBLANES - 1:SUBLANES, :]
            h_groups.append(h_grp)
        lru["carry"] = carry
        h = jnp.concatenate(h_groups, axis=0)
        gb = u[rows, o1 + D_LRU:o1 + 2 * D_LRU]
        y[rows, D_CONV:D_CONV + D_LRU] = (h * jax.nn.gelu(gb)).astype(BF16)

    k_dup = kv[0:N_KV_HEADS]
    v_lo = kv[N_KV_HEADS:2 * N_KV_HEADS]
    v_hi = kv[2 * N_KV_HEADS:3 * N_KV_HEADS]
    heads_per_kv = N_Q_HEADS // N_KV_HEADS

    def kv_prep():
        kk = u[:, o3:o3 + KV_DIM]
        vv = u[:, o3 + KV_DIM:o3 + 2 * KV_DIM]
        lo = lax.broadcasted_iota(jnp.int32, (T, LANES), 1) < HEAD_DIM
        k_rot = pltpu.roll(kk, HEAD_DIM, 1)
        v_rot = pltpu.roll(vv, HEAD_DIM, 1)
        new_rows = slice(BLOCK, BLOCK + T)
        k_dup[0][new_rows, :] = jnp.where(lo, kk, k_rot).astype(BF16)
        k_dup[1][new_rows, :] = jnp.where(lo, k_rot, kk).astype(BF16)
        v_lo[0][new_rows, :] = jnp.where(lo, vv, 0.0).astype(BF16)
        v_hi[0][new_rows, :] = jnp.where(lo, 0.0, v_rot).astype(BF16)
        v_lo[1][new_rows, :] = jnp.where(lo, v_rot, 0.0).astype(BF16)
        v_hi[1][new_rows, :] = jnp.where(lo, 0.0, vv).astype(BF16)

    def attn(qb, grp):
        qi = lax.broadcasted_iota(jnp.int32, (BLOCK, 2 * BLOCK), 0)
        kj = lax.broadcasted_iota(jnp.int32, (BLOCK, 2 * BLOCK), 1)
        dist = qi - kj + BLOCK
        valid = pltpu.bitcast(dist, jnp.uint32) < WINDOW
        if qb == 0:
            valid = valid & (kj >= jnp.where(s_idx == 0, BLOCK, 0))
        qlane_lo = lax.broadcasted_iota(jnp.int32, (BLOCK, LANES), 1) < HEAD_DIM
        r0 = qb * BLOCK
        q_rows = slice(r0, r0 + BLOCK)
        band = slice(r0, r0 + 2 * BLOCK)
        chunks = range(grp * heads_per_kv // 2, (grp + 1) * heads_per_kv // 2)
        q_parts = []
        for c in chunks:
            qc = u[q_rows, o2 + c * LANES:o2 + (c + 1) * LANES] * (HEAD_DIM ** -0.5)
            q_parts += [jnp.where(qlane_lo, qc, 0.0).astype(BF16), jnp.where(qlane_lo, 0.0, qc).astype(BF16)]
        s_all = _dot_nt(jnp.concatenate(q_parts, axis=0), k_dup[grp][band, :])
        probs = []
        for hh in range(heads_per_kv):
            head = grp * heads_per_kv + hh
            s = s_all[hh * BLOCK:(hh + 1) * BLOCK, :] + bias_tab[head]
            s = jnp.where(valid, s, -1e30)
            sink = sink_ref[layer, head]
            m = jnp.maximum(jnp.max(s, axis=-1, keepdims=True), sink)
            e = jnp.exp(s - m)
            den = jnp.sum(e, axis=-1, keepdims=True) + jnp.exp(sink - m)
            probs.append((e / den).astype(BF16))
        p_rows = [jnp.concatenate(probs[2 * i:2 * i + 2], axis=1) for i in range(heads_per_kv // 2)]
        v_both = jnp.concatenate([v_lo[grp][band, :], v_hi[grp][band, :]], axis=0)
        o_pairs = _dot(jnp.concatenate(p_rows, axis=0), v_both)
        for i, c in enumerate(chunks):
            col = D_CONV + D_LRU + c * LANES
            y[q_rows, col:col + LANES] = o_pairs[i * BLOCK:(i + 1) * BLOCK, :].astype(BF16)

    def out_proj(qb):
        q_rows = slice(qb * BLOCK, (qb + 1) * BLOCK)
        hproj = _dot(y[q_rows, :], wout_ref[...])
        o_ref[0, q_rows, :] = _layernorm(ALPHA * x_ref[0, q_rows, :] + hproj, g_ref[...], b_ref[...])

    in_proj()
    conv_prep()
    for rb in range(n_rb):
        conv_block(rb)
    lru_prep()
    for rb in range(n_rb):
        lru_block(rb)
    kv_prep()
    for qb in range(T // BLOCK):
        for grp in range(N_KV_HEADS):
            attn(qb, grp)
        out_proj(qb)
    hcar[...] = lru["carry"]
    for buf in kv:
        buf[0:BLOCK, :] = buf[T:T + BLOCK, :]


def _mixer_call(x, layer, w_in, w_out, bucket, rel_bias, cw, cb, cg, cbeta, lw, lb, wa, ba, wx, bx, lam,
                sinks, ln_g, ln_b):
    B, S, _ = x.shape
    T = SEQ_TILE
    x_spec = pl.BlockSpec((1, T, D_MODEL), lambda bi, si: (bi, si, 0))
    lay = (layer,)
    smem = pl.BlockSpec(memory_space=pltpu.SMEM)
    in_specs = [
        x_spec,
        _resident((D_MODEL, D_IN), lay),
        _resident((D_MODEL, D_MODEL), lay),
        _resident((BLOCK, 2 * BLOCK)),
        smem,
        _resident((CONV_KERNEL, D_CONV), lay), _resident((1, D_CONV), lay),
        _resident((1, D_CONV), lay), _resident((1, D_CONV), lay),
        _resident((LRU_CONV, D_LRU), lay), _resident((1, D_LRU), lay),
        _resident((D_LRU, D_LRU), lay), _resident((1, D_LRU), lay),
        _resident((D_LRU, D_LRU), lay), _resident((1, D_LRU), lay),
        _resident((1, D_LRU), lay),
        smem,
        _resident((1, D_MODEL), (layer, 1)), _resident((1, D_MODEL), (layer, 1)),
    ]
    scratch = (
        [pltpu.VMEM((N_Q_HEADS, BLOCK, 2 * BLOCK), F32),
         pltpu.VMEM((CONV_HIST, D_CONV), F32),
         pltpu.VMEM((LRU_HIST, D_LRU), F32),
         pltpu.VMEM((1, D_LRU), F32),
         pltpu.VMEM((T, D_IN), F32),
         pltpu.VMEM((T, D_MODEL), BF16),
         pltpu.VMEM((SUBLANES, CONV_HIST + T, D_CONV), F32),
         pltpu.VMEM((LRU_HIST + T, D_LRU), F32)]
        + [pltpu.VMEM((BLOCK + T, KV_DIM), BF16) for _ in range(N_KV_BUFS)])
    return pl.pallas_call(
        functools.partial(_mixer_kernel, layer),
        grid=(B, S // T),
        in_specs=in_specs,
        out_specs=x_spec,
        out_shape=jax.ShapeDtypeStruct(x.shape, F32),
        scratch_shapes=scratch,
        compiler_params=pltpu.CompilerParams(
            dimension_semantics=("arbitrary", "arbitrary"), vmem_limit_bytes=VMEM_LIMIT_BYTES),
        name="mixer_ln",
    )(x, w_in, w_out, bucket, rel_bias, cw, cb, cg, cbeta, lw, lb, wa, ba, wx, bx, lam, sinks, ln_g, ln_b)


def _bucket_table():
    qi = np.arange(BLOCK)[:, None]
    kj = np.arange(2 * BLOCK)[None, :]
    dist = np.maximum(qi - kj + BLOCK, 0)
    max_exact = REL_BUCKETS // 2
    ratio = np.log(np.maximum(dist, 1).astype(np.float32) / np.float32(max_exact)) / np.float32(
        math.log(REL_MAX_DIST / max_exact))
    large = max_exact + (ratio * np.float32(REL_BUCKETS - max_exact)).astype(np.int32)
    return np.where(dist < max_exact, dist, np.minimum(large, REL_BUCKETS - 1)).astype(np.int32)


def _block_diag(w):
    l, h, d, _ = w.shape
    eye = jnp.eye(h, dtype=w.dtype)
    return (eye[None, :, None, :, None] * w[:, :, :, None, :]).reshape(l, h * d, h * d)


def kernel(x, rel_bias, ln_g, ln_b, ffn_w_gate, ffn_w_up, ffn_w_down, w_in, conv_dw_w, conv_dw_b, conv_ln_g, conv_ln_b, lru_conv_w, lru_conv_b, lru_wa, lru_ba, lru_wx, lru_bx, lru_lambda, attn_sinks, w_out):
    B, S, D = x.shape
    assert D == D_MODEL and S % SEQ_TILE == 0 and (B * S) % FFN_TILE == 0
    rows = lambda v: v.reshape(v.shape[:-1] + (1, v.shape[-1]))
    wg, wu, wd = ffn_w_gate.astype(BF16), ffn_w_up.astype(BF16), ffn_w_down.astype(BF16)
    w_in_b, w_out_b = w_in.astype(BF16), w_out.astype(BF16)
    wa, wx = _block_diag(lru_wa).astype(BF16), _block_diag(lru_wx).astype(BF16)
    g4, b4 = rows(ln_g), rows(ln_b)
    bucket = _bucket_table()
    for l in range(DEPTH):
        x = _ffn_call(x.reshape(B * S, D), wg, wu, wd, g4, b4, l, 0).reshape(B, S, D)
        x = _mixer_call(
            x, l, w_in_b, w_out_b, bucket, rel_bias,
            conv_dw_w, rows(conv_dw_b), rows(conv_ln_g), rows(conv_ln_b),
            lru_conv_w, rows(lru_conv_b), wa, rows(lru_ba), wx, rows(lru_bx), rows(lru_lambda),
            attn_sinks, g4, b4)
        x = _ffn_call(x.reshape(B * S, D), wg, wu, wd, g4, b4, l, 1).reshape(B, S, D)
    return x
```

```python
import functools
import math

import jax
import jax.numpy as jnp
import numpy as np
from jax import lax
from jax.experimental import pallas as pl
from jax.experimental.pallas import tpu as pltpu

D_MODEL = 1024
DEPTH = 2
D_CONV = 256
D_LRU = 256
D_ATTN = 512
CONV_KERNEL = 31
LRU_HEADS = 4
LRU_HEAD_DIM = 64
LRU_CONV = 4
LRU_C = 8.0
HEAD_DIM = 64
N_Q_HEADS = 8
N_KV_HEADS = 2
KV_DIM = 128
WINDOW = 128
BLOCK = 128
REL_BUCKETS = 32
REL_MAX_DIST = 128
D_FF = 2816
ALPHA = (2.0 * DEPTH) ** 0.25
LN_EPS = 1e-5
D_IN = 2 * D_CONV + 2 * D_LRU + D_ATTN + 2 * KV_DIM

LANES = 128
SUBLANES = 8
FF_CHUNK = 256
FFN_TILE = 512
SEQ_TILE = 512
ROW_BLOCK = 64
CONV_HIST = 32
LRU_HIST = 8
N_KV_BUFS = 6
VMEM_LIMIT_BYTES = 56 * 1024 * 1024

F32 = jnp.float32
BF16 = jnp.bfloat16


def _layernorm(z, g, b):
    mu = jnp.mean(z, axis=-1, keepdims=True)
    zc = z - mu
    var = jnp.mean(zc * zc, axis=-1, keepdims=True)
    return zc * lax.rsqrt(var + LN_EPS) * g + b


def _dot(a, b):
    return jnp.dot(a, b, preferred_element_type=F32)


def _dot_nt(a, b):
    return lax.dot_general(a, b, (((1,), (1,)), ((), ())), preferred_element_type=F32)


def _resident(shape, lead=()):
    block = (None,) * len(lead) + tuple(shape)
    index = tuple(lead) + (0,) * len(shape)
    return pl.BlockSpec(block, lambda *_: index, pipeline_mode=pl.Buffered(1))


def _ffn_kernel(n_tiles, layer, half,
                x_ref, wg_hbm, wu_hbm, wd_hbm, g_ref, b_ref,
                o_ref,
                h_ref, z_ref, wg_ref, wu_ref, wd_ref, stage_g, stage_u, stage_d, sems):
    i = pl.program_id(0)
    n_chunks = D_FF // FF_CHUNK

    def normalise_previous():
        out = _layernorm(z_ref[...], g_ref[...], b_ref[...])
        o_ref[...] = out
        return out

    def swiglu_chunk(xb, c, extra=None):
        sl = slice(c * FF_CHUNK, (c + 1) * FF_CHUNK)
        gate = _dot(xb, wg_ref[:, sl])
        up = _dot(xb, wu_ref[:, sl])
        if extra is not None:
            up = up + extra
        h_ref[:, sl] = (gate * jax.nn.sigmoid(gate) * up).astype(BF16)

    def down_and_residual(x):
        y = _dot(h_ref[...], wd_ref[...])
        z_ref[...] = ALPHA * x + 0.5 * y

    def weight_copies(c, slot):
        cols = pl.ds(c * FF_CHUNK, FF_CHUNK)
        return (
            pltpu.make_async_copy(wg_hbm.at[layer, half, :, cols], stage_g.at[slot], sems.at[slot, 0]),
            pltpu.make_async_copy(wu_hbm.at[layer, half, :, cols], stage_u.at[slot], sems.at[slot, 1]),
            pltpu.make_async_copy(wd_hbm.at[layer, half, cols, :], stage_d.at[slot], sems.at[slot, 2]),
        )

    @pl.when(i == 0)
    def _():
        for cp in weight_copies(0, 0):
            cp.start()
        x = x_ref[...]
        xb = x.astype(BF16)
        for c in range(n_chunks):
            slot = c % 2
            if c + 1 < n_chunks:
                for cp in weight_copies(c + 1, 1 - slot):
                    cp.start()
            for cp in weight_copies(c, slot):
                cp.wait()
            sl = slice(c * FF_CHUNK, (c + 1) * FF_CHUNK)
            wg_ref[:, sl] = stage_g[slot].astype(BF16)
            wu_ref[:, sl] = stage_u[slot].astype(BF16)
            wd_ref[sl, :] = stage_d[slot].astype(BF16)
            swiglu_chunk(xb, c)
        down_and_residual(x)

    @pl.when((i > 0) & (i < n_tiles))
    def _():
        out_prev = normalise_previous()
        bits = pltpu.bitcast(out_prev, jnp.uint32)
        acc = bits[0:SUBLANES, :]
        for grp in range(1, FFN_TILE // SUBLANES):
            acc = acc | bits[grp * SUBLANES:(grp + 1) * SUBLANES, :]
        word = acc[:, 0:LANES]
        for blk in range(1, D_MODEL // LANES):
            word = word | acc[:, blk * LANES:(blk + 1) * LANES]
        zero = pltpu.bitcast(lax.shift_right_logical(lax.shift_right_logical(word, jnp.uint32(16)), jnp.uint32(16)), F32)
        zero_rows = jnp.broadcast_to(zero[0:1, 0:1], (FFN_TILE, FF_CHUNK))
        x = x_ref[...]
        xb = x.astype(BF16)
        for c in range(n_chunks):
            swiglu_chunk(xb, c, zero_rows if c == n_chunks - 1 else None)
        down_and_residual(x)

    @pl.when(i == n_tiles)
    def _():
        normalise_previous()


def _ffn_call(x2d, wg, wu, wd, ln_g, ln_b, layer, half):
    n_tok = x2d.shape[0]
    n_tiles = n_tok // FFN_TILE
    hbm = pl.BlockSpec(memory_space=pl.ANY)
    return pl.pallas_call(
        functools.partial(_ffn_kernel, n_tiles, layer, half),
        grid=(n_tiles + 1,),
        in_specs=[
            pl.BlockSpec((FFN_TILE, D_MODEL), lambda i: (jnp.minimum(i, n_tiles - 1), 0)),
            hbm, hbm, hbm,
            _resident((1, D_MODEL), (layer, 2 * half)),
            _resident((1, D_MODEL), (layer, 2 * half)),
        ],
        out_specs=pl.BlockSpec((FFN_TILE, D_MODEL), lambda i: (jnp.maximum(i - 1, 0), 0)),
        out_shape=jax.ShapeDtypeStruct((n_tok, D_MODEL), F32),
        scratch_shapes=[
            pltpu.VMEM((FFN_TILE, D_FF), BF16),
            pltpu.VMEM((FFN_TILE, D_MODEL), F32),
            pltpu.VMEM((D_MODEL, D_FF), BF16),
            pltpu.VMEM((D_MODEL, D_FF), BF16),
            pltpu.VMEM((D_FF, D_MODEL), BF16),
            pltpu.VMEM((2, D_MODEL, FF_CHUNK), F32),
            pltpu.VMEM((2, D_MODEL, FF_CHUNK), F32),
            pltpu.VMEM((2, FF_CHUNK, D_MODEL), F32),
            pltpu.SemaphoreType.DMA((2, 3)),
        ],
        compiler_params=pltpu.CompilerParams(
            dimension_semantics=("arbitrary",), vmem_limit_bytes=VMEM_LIMIT_BYTES),
        name="ffn_ln",
    )(x2d, wg, wu, wd, ln_g, ln_b)


def _mixer_kernel(layer,
                  x_ref, win_ref, wout_ref, bucket_ref, relb_ref,
                  cw_ref, cb_ref, cg_ref, cbeta_ref,
                  lw_ref, lb_ref, wa_ref, ba_ref, wx_ref, bx_ref, lam_ref,
                  sink_ref, g_ref, b_ref,
                  o_ref,
                  bias_tab, chist, xhist, hcar, u, y, cs, xb, *kv):
    T = SEQ_TILE
    s_idx = pl.program_id(1)

    @pl.when((pl.program_id(0) == 0) & (s_idx == 0))
    def _build_bias_table():
        bucket = bucket_ref[...]
        for head in range(N_Q_HEADS):
            tab = jnp.zeros((BLOCK, 2 * BLOCK), F32)
            for bkt in range(REL_BUCKETS):
                tab = jnp.where(bucket == bkt, relb_ref[bkt, head], tab)
            bias_tab[head] = tab

    @pl.when(s_idx == 0)
    def _start_of_sequence():
        chist[...] = jnp.zeros((CONV_HIST, D_CONV), F32)
        xhist[...] = jnp.zeros((LRU_HIST, D_LRU), F32)
        hcar[...] = jnp.zeros((1, D_LRU), F32)
        for buf in kv:
            buf[0:BLOCK, :] = jnp.zeros((BLOCK, KV_DIM), BF16)

    o1 = 2 * D_CONV
    o2 = o1 + 2 * D_LRU
    o3 = o2 + D_ATTN
    n_rb = T // ROW_BLOCK

    def in_proj():
        x_bf = x_ref[0].astype(BF16)
        for c0, c1 in ((0, o1), (o1, o2), (o3, D_IN), (o2, o3)):
            u[:, c0:c1] = _dot(x_bf, win_ref[:, c0:c1])

    def conv_prep():
        yglu = u[:, 0:D_CONV] * jax.nn.sigmoid(u[:, D_CONV:2 * D_CONV])
        n_rows = CONV_HIST + T
        whole = jnp.concatenate([chist[...], yglu], axis=0)
        cs[0] = whole
        for r in range(1, SUBLANES):
            cs[r] = pltpu.roll(whole, n_rows - r, 0)
        chist[...] = yglu[T - CONV_HIST:T, :]

    def conv_block(rb):
        off = CONV_HIST - (CONV_KERNEL - 1)
        acc = jnp.broadcast_to(cb_ref[...], (ROW_BLOCK, D_CONV))
        for tap in range(CONV_KERNEL):
            r = (off + tap) % SUBLANES
            first = rb * ROW_BLOCK + off + tap - r
            acc = acc + cw_ref[tap:tap + 1, :] * cs[r, first:first + ROW_BLOCK, :]
        yc = _layernorm(acc, cg_ref[...], cbeta_ref[...])
        y[rb * ROW_BLOCK:(rb + 1) * ROW_BLOCK, 0:D_CONV] = (yc * jax.nn.sigmoid(yc)).astype(BF16)

    lru = {}

    def lru_prep():
        lam = lam_ref[...]
        lru["log_sig"] = -(jnp.maximum(-lam, 0.0) + jnp.log1p(jnp.exp(-jnp.abs(lam))))
        x_new = u[:, o1:o1 + D_LRU]
        xb[0:LRU_HIST, :] = xhist[...]
        xb[LRU_HIST:LRU_HIST + T, :] = x_new
        xhist[...] = x_new[T - LRU_HIST:T, :]
        lru["carry"] = hcar[...]

    def lru_block(rb):
        off = LRU_HIST - (LRU_CONV - 1)
        halo = ROW_BLOCK + SUBLANES
        n_grp = ROW_BLOCK // SUBLANES
        sub = lax.broadcasted_iota(jnp.int32, (n_grp, SUBLANES, D_LRU), 1)
        rows = slice(rb * ROW_BLOCK, (rb + 1) * ROW_BLOCK)
        blk = xb[rb * ROW_BLOCK:rb * ROW_BLOCK + halo, :]
        xc = jnp.broadcast_to(lb_ref[...], (ROW_BLOCK, D_LRU))
        for tap in range(LRU_CONV):
            r = (off + tap) % SUBLANES
            first = off + tap - r
            src = blk if r == 0 else pltpu.roll(blk, halo - r, 0)
            xc = xc + lw_ref[tap:tap + 1, :] * src[first:first + ROW_BLOCK, :]
        xcb = xc.astype(BF16)
        r_gate = jax.nn.sigmoid(_dot(xcb, wa_ref[...]) + ba_ref[...])
        i_gate = jax.nn.sigmoid(_dot(xcb, wx_ref[...]) + bx_ref[...])
        log_a = LRU_C * r_gate * lru["log_sig"]
        a = jnp.exp(log_a)
        mult = jnp.sqrt(-jnp.tanh(log_a) * (a * a + 1.0))
        bterm = mult * (i_gate * xc)
        a3 = a.reshape(n_grp, SUBLANES, D_LRU)
        b3 = bterm.reshape(n_grp, SUBLANES, D_LRU)
        k = 1
        while k < SUBLANES:
            a_prev = jnp.where(sub >= k, pltpu.roll(a3, k, 1), 1.0)
            b_prev = jnp.where(sub >= k, pltpu.roll(b3, k, 1), 0.0)
            b3 = a3 * b_prev + b3
            a3 = a3 * a_prev
            k *= 2
        carry = lru["carry"]
        h_groups = []
        for grp_i in range(n_grp):
            h_grp = a3[grp_i] * carry + b3[grp_i]
            carry = h_grp[SUBLANES - 1:SUBLANES, :]
            h_groups.append(h_grp)
        lru["carry"] = carry
        h = jnp.concatenate(h_groups, axis=0)
        gb = u[rows, o1 + D_LRU:o1 + 2 * D_LRU]
        y[rows, D_CONV:D_CONV + D_LRU] = (h * jax.nn.gelu(gb)).astype(BF16)

    k_dup = kv[0:N_KV_HEADS]
    v_lo = kv[N_KV_HEADS:2 * N_KV_HEADS]
    v_hi = kv[2 * N_KV_HEADS:3 * N_KV_HEADS]
    heads_per_kv = N_Q_HEADS // N_KV_HEADS

    def kv_prep():
        kk = u[:, o3:o3 + KV_DIM]
        vv = u[:, o3 + KV_DIM:o3 + 2 * KV_DIM]
        lo = lax.broadcasted_iota(jnp.int32, (T, LANES), 1) < HEAD_DIM
        k_rot = pltpu.roll(kk, HEAD_DIM, 1)
        v_rot = pltpu.roll(vv, HEAD_DIM, 1)
        new_rows = slice(BLOCK, BLOCK + T)
        k_dup[0][new_rows, :] = jnp.where(lo, kk, k_rot).astype(BF16)
        k_dup[1][new_rows, :] = jnp.where(lo, k_rot, kk).astype(BF16)
        v_lo[0][new_rows, :] = jnp.where(lo, vv, 0.0).astype(BF16)
        v_hi[0][new_rows, :] = jnp.where(lo, 0.0, v_rot).astype(BF16)
        v_lo[1][new_rows, :] = jnp.where(lo, v_rot, 0.0).astype(BF16)
        v_hi[1][new_rows, :] = jnp.where(lo, 0.0, vv).astype(BF16)

    def attn(qb, grp):
        qi = lax.broadcasted_iota(jnp.int32, (BLOCK, 2 * BLOCK), 0)
        kj = lax.broadcasted_iota(jnp.int32, (BLOCK, 2 * BLOCK), 1)
        dist = qi - kj + BLOCK
        valid = pltpu.bitcast(dist, jnp.uint32) < WINDOW
        if qb == 0:
            valid = valid & (kj >= jnp.where(s_idx == 0, BLOCK, 0))
        qlane_lo = lax.broadcasted_iota(jnp.int32, (BLOCK, LANES), 1) < HEAD_DIM
        r0 = qb * BLOCK
        q_rows = slice(r0, r0 + BLOCK)
        band = slice(r0, r0 + 2 * BLOCK)
        chunks = range(grp * heads_per_kv // 2, (grp + 1) * heads_per_kv // 2)
        q_parts = []
        for c in chunks:
            qc = u[q_rows, o2 + c * LANES:o2 + (c + 1) * LANES] * (HEAD_DIM ** -0.5)
            q_parts += [jnp.where(qlane_lo, qc, 0.0).astype(BF16), jnp.where(qlane_lo, 0.0, qc).astype(BF16)]
        s_all = _dot_nt(jnp.concatenate(q_parts, axis=0), k_dup[grp][band, :])
        probs = []
        for hh in range(heads_per_kv):
            head = grp * heads_per_kv + hh
            s = s_all[hh * BLOCK:(hh + 1) * BLOCK, :] + bias_tab[head]
            s = jnp.where(valid, s, -1e30)
            sink = sink_ref[layer, head]
            m = jnp.maximum(jnp.max(s, axis=-1, keepdims=True), sink)
            e = jnp.exp(s - m)
            den = jnp.sum(e, axis=-1, keepdims=True) + jnp.exp(sink - m)
            probs.append((e / den).astype(BF16))
        p_rows = [jnp.concatenate(probs[2 * i:2 * i + 2], axis=1) for i in range(heads_per_kv // 2)]
        v_both = jnp.concatenate([v_lo[grp][band, :], v_hi[grp][band, :]], axis=0)
        o_pairs = _dot(jnp.concatenate(p_rows, axis=0), v_both)
        for i, c in enumerate(chunks):
            col = D_CONV + D_LRU + c * LANES
            y[q_rows, col:col + LANES] = o_pairs[i * BLOCK:(i + 1) * BLOCK, :].astype(BF16)

    def out_proj(qb):
        q_rows = slice(qb * BLOCK, (qb + 1) * BLOCK)
        hproj = _dot(y[q_rows, :], wout_ref[...])
        o_ref[0, q_rows, :] = _layernorm(ALPHA * x_ref[0, q_rows, :] + hproj, g_ref[...], b_ref[...])

    in_proj()
    conv_prep()
    for rb in range(n_rb):
        conv_block(rb)
    lru_prep()
    for rb in range(n_rb):
        lru_block(rb)
    kv_prep()
    for qb in range(T // BLOCK):
        for grp in range(N_KV_HEADS):
            attn(qb, grp)
        out_proj(qb)
    hcar[...] = lru["carry"]
    for buf in kv:
        buf[0:BLOCK, :] = buf[T:T + BLOCK, :]


def _mixer_call(x, layer, w_in, w_out, bucket, rel_bias, cw, cb, cg, cbeta, lw, lb, wa, ba, wx, bx, lam,
                sinks, ln_g, ln_b):
    B, S, _ = x.shape
    T = SEQ_TILE
    x_spec = pl.BlockSpec((1, T, D_MODEL), lambda bi, si: (bi, si, 0))
    lay = (layer,)
    smem = pl.BlockSpec(memory_space=pltpu.SMEM)
    in_specs = [
        x_spec,
        _resident((D_MODEL, D_IN), lay),
        _resident((D_MODEL, D_MODEL), lay),
        _resident((BLOCK, 2 * BLOCK)),
        smem,
        _resident((CONV_KERNEL, D_CONV), lay), _resident((1, D_CONV), lay),
        _resident((1, D_CONV), lay), _resident((1, D_CONV), lay),
        _resident((LRU_CONV, D_LRU), lay), _resident((1, D_LRU), lay),
        _resident((D_LRU, D_LRU), lay), _resident((1, D_LRU), lay),
        _resident((D_LRU, D_LRU), lay), _resident((1, D_LRU), lay),
        _resident((1, D_LRU), lay),
        smem,
        _resident((1, D_MODEL), (layer, 1)), _resident((1, D_MODEL), (layer, 1)),
    ]
    scratch = (
        [pltpu.VMEM((N_Q_HEADS, BLOCK, 2 * BLOCK), F32),
         pltpu.VMEM((CONV_HIST, D_CONV), F32),
         pltpu.VMEM((LRU_HIST, D_LRU), F32),
         pltpu.VMEM((1, D_LRU), F32),
         pltpu.VMEM((T, D_IN), F32),
         pltpu.VMEM((T, D_MODEL), BF16),
         pltpu.VMEM((SUBLANES, CONV_HIST + T, D_CONV), F32),
         pltpu.VMEM((LRU_HIST + T, D_LRU), F32)]
        + [pltpu.VMEM((BLOCK + T, KV_DIM), BF16) for _ in range(N_KV_BUFS)])
    return pl.pallas_call(
        functools.partial(_mixer_kernel, layer),
        grid=(B, S // T),
        in_specs=in_specs,
        out_specs=x_spec,
        out_shape=jax.ShapeDtypeStruct(x.shape, F32),
        scratch_shapes=scratch,
        compiler_params=pltpu.CompilerParams(
            dimension_semantics=("arbitrary", "arbitrary"), vmem_limit_bytes=VMEM_LIMIT_BYTES),
        name="mixer_ln",
    )(x, w_in, w_out, bucket, rel_bias, cw, cb, cg, cbeta, lw, lb, wa, ba, wx, bx, lam, sinks, ln_g, ln_b)


def _bucket_table():
    qi = np.arange(BLOCK)[:, None]
    kj = np.arange(2 * BLOCK)[None, :]
    dist = np.maximum(qi - kj + BLOCK, 0)
    max_exact = REL_BUCKETS // 2
    ratio = np.log(np.maximum(dist, 1).astype(np.float32) / np.float32(max_exact)) / np.float32(
        math.log(REL_MAX_DIST / max_exact))
    large = max_exact + (ratio * np.float32(REL_BUCKETS - max_exact)).astype(np.int32)
    return np.where(dist < max_exact, dist, np.minimum(large, REL_BUCKETS - 1)).astype(np.int32)


def _block_diag(w):
    l, h, d, _ = w.shape
    eye = jnp.eye(h, dtype=w.dtype)
    return (eye[None, :, None, :, None] * w[:, :, :, None, :]).reshape(l, h * d, h * d)


def kernel(x, rel_bias, ln_g, ln_b, ffn_w_gate, ffn_w_up, ffn_w_down, w_in, conv_dw_w, conv_dw_b, conv_ln_g, conv_ln_b, lru_conv_w, lru_conv_b, lru_wa, lru_ba, lru_wx, lru_bx, lru_lambda, attn_sinks, w_out):
    B, S, D = x.shape
    assert D == D_MODEL and S % SEQ_TILE == 0 and (B * S) % FFN_TILE == 0
    rows = lambda v: v.reshape(v.shape[:-1] + (1, v.shape[-1]))
    wg, wu, wd = ffn_w_gate, ffn_w_up, ffn_w_down
    w_in_b, w_out_b = w_in.astype(BF16), w_out.astype(BF16)
    wa, wx = _block_diag(lru_wa).astype(BF16), _block_diag(lru_wx).astype(BF16)
    g4, b4 = rows(ln_g), rows(ln_b)
    bucket = _bucket_table()
    for l in range(DEPTH):
        x = _ffn_call(x.reshape(B * S, D), wg, wu, wd, g4, b4, l, 0).reshape(B, S, D)
        x = _mixer_call(
            x, l, w_in_b, w_out_b, bucket, rel_bias,
            conv_dw_w, rows(conv_dw_b), rows(conv_ln_g), rows(conv_ln_b),
            lru_conv_w, rows(lru_conv_b), wa, rows(lru_ba), wx, rows(lru_bx), rows(lru_lambda),
            attn_sinks, g4, b4)
        x = _ffn_call(x.reshape(B * S, D), wg, wu, wd, g4, b4, l, 1).reshape(B, S, D)
    return x
```

```python
import functools
import math

import jax
import jax.numpy as jnp
import numpy as np
from jax import lax
from jax.experimental import pallas as pl
from jax.experimental.pallas import tpu as pltpu

D_MODEL = 1024
DEPTH = 2
D_CONV = 256
D_LRU = 256
D_ATTN = 512
CONV_KERNEL = 31
LRU_HEADS = 4
LRU_HEAD_DIM = 64
LRU_CONV = 4
LRU_C = 8.0
HEAD_DIM = 64
N_Q_HEADS = 8
N_KV_HEADS = 2
KV_DIM = 128
WINDOW = 128
BLOCK = 128
REL_BUCKETS = 32
REL_MAX_DIST = 128
D_FF = 2816
ALPHA = (2.0 * DEPTH) ** 0.25
LN_EPS = 1e-5
D_IN = 2 * D_CONV + 2 * D_LRU + D_ATTN + 2 * KV_DIM

LANES = 128
SUBLANES = 8
FF_CHUNK = 256
FFN_TILE = 1024
SEQ_TILE = 512
ROW_BLOCK = 64
CONV_HIST = 32
LRU_HIST = 8
N_KV_BUFS = 8
VMEM_LIMIT_BYTES = 56 * 1024 * 1024

F32 = jnp.float32
BF16 = jnp.bfloat16


def _layernorm(z, g, b):
    mu = jnp.mean(z, axis=-1, keepdims=True)
    zc = z - mu
    var = jnp.mean(zc * zc, axis=-1, keepdims=True)
    return zc * lax.rsqrt(var + LN_EPS) * g + b


def _gelu_tanh(x):
    z2 = (2.0 * math.sqrt(2.0 / math.pi)) * (x + 0.044715 * (x * x * x))
    return x * jax.nn.sigmoid(z2)


def _dot(a, b):
    return jnp.dot(a, b, preferred_element_type=F32)


def _dot_nt(a, b):
    return lax.dot_general(a, b, (((1,), (1,)), ((), ())), preferred_element_type=F32)


def _resident(shape, lead=()):
    block = (None,) * len(lead) + tuple(shape)
    index = tuple(lead) + (0,) * len(shape)
    return pl.BlockSpec(block, lambda *_: index, pipeline_mode=pl.Buffered(1))


def _ffn_kernel(n_tiles, layer, half,
                x_ref, wg_hbm, wu_hbm, wd_hbm, g_ref, b_ref,
                o_ref,
                h_ref, z_ref, wg_ref, wu_ref, wd_ref, stage_g, stage_u, stage_d, sems):
    i = pl.program_id(0)
    n_chunks = D_FF // FF_CHUNK

    def normalise_previous():
        out = _layernorm(z_ref[...], g_ref[...], b_ref[...])
        o_ref[...] = out
        return out

    def swiglu_chunk(xb, c, extra=None):
        sl = slice(c * FF_CHUNK, (c + 1) * FF_CHUNK)
        gate = _dot(xb, wg_ref[:, sl])
        up = _dot(xb, wu_ref[:, sl])
        if extra is not None:
            up = up + extra
        h_ref[:, sl] = (gate * jax.nn.sigmoid(gate) * up).astype(BF16)

    def down_and_residual(x):
        y = _dot(h_ref[...], wd_ref[...])
        z_ref[...] = ALPHA * x + 0.5 * y

    def weight_copies(c, slot):
        cols = pl.ds(c * FF_CHUNK, FF_CHUNK)
        return (
            pltpu.make_async_copy(wg_hbm.at[layer, half, :, cols], stage_g.at[slot], sems.at[slot, 0]),
            pltpu.make_async_copy(wu_hbm.at[layer, half, :, cols], stage_u.at[slot], sems.at[slot, 1]),
            pltpu.make_async_copy(wd_hbm.at[layer, half, cols, :], stage_d.at[slot], sems.at[slot, 2]),
        )

    @pl.when(i == 0)
    def _():
        for cp in weight_copies(0, 0):
            cp.start()
        x = x_ref[...]
        xb = x.astype(BF16)
        for c in range(n_chunks):
            slot = c % 2
            if c + 1 < n_chunks:
                for cp in weight_copies(c + 1, 1 - slot):
                    cp.start()
            for cp in weight_copies(c, slot):
                cp.wait()
            sl = slice(c * FF_CHUNK, (c + 1) * FF_CHUNK)
            wg_ref[:, sl] = stage_g[slot].astype(BF16)
            wu_ref[:, sl] = stage_u[slot].astype(BF16)
            wd_ref[sl, :] = stage_d[slot].astype(BF16)
            swiglu_chunk(xb, c)
        down_and_residual(x)

    @pl.when((i > 0) & (i < n_tiles))
    def _():
        out_prev = normalise_previous()
        bits = pltpu.bitcast(out_prev, jnp.uint32)
        acc = bits[0:SUBLANES, :]
        for grp in range(1, FFN_TILE // SUBLANES):
            acc = acc | bits[grp * SUBLANES:(grp + 1) * SUBLANES, :]
        word = acc[:, 0:LANES]
        for blk in range(1, D_MODEL // LANES):
            word = word | acc[:, blk * LANES:(blk + 1) * LANES]
        zero = pltpu.bitcast(lax.shift_right_logical(lax.shift_right_logical(word, jnp.uint32(16)), jnp.uint32(16)), F32)
        zero_rows = jnp.broadcast_to(zero[0:1, 0:1], (FFN_TILE, FF_CHUNK))
        x = x_ref[...]
        xb = x.astype(BF16)
        for c in range(n_chunks):
            swiglu_chunk(xb, c, zero_rows if c == n_chunks - 1 else None)
        down_and_residual(x)

    @pl.when(i == n_tiles)
    def _():
        normalise_previous()


def _ffn_call(x2d, wg, wu, wd, ln_g, ln_b, layer, half):
    n_tok = x2d.shape[0]
    n_tiles = n_tok // FFN_TILE
    hbm = pl.BlockSpec(memory_space=pl.ANY)
    return pl.pallas_call(
        functools.partial(_ffn_kernel, n_tiles, layer, half),
        grid=(n_tiles + 1,),
        in_specs=[
            pl.BlockSpec((FFN_TILE, D_MODEL), lambda i: (jnp.minimum(i, n_tiles - 1), 0)),
            hbm, hbm, hbm,
            _resident((1, D_MODEL), (layer, 2 * half)),
            _resident((1, D_MODEL), (layer, 2 * half)),
        ],
        out_specs=pl.BlockSpec((FFN_TILE, D_MODEL), lambda i: (jnp.maximum(i - 1, 0), 0)),
        out_shape=jax.ShapeDtypeStruct((n_tok, D_MODEL), F32),
        scratch_shapes=[
            pltpu.VMEM((FFN_TILE, D_FF), BF16),
            pltpu.VMEM((FFN_TILE, D_MODEL), F32),
            pltpu.VMEM((D_MODEL, D_FF), BF16),
            pltpu.VMEM((D_MODEL, D_FF), BF16),
            pltpu.VMEM((D_FF, D_MODEL), BF16),
            pltpu.VMEM((2, D_MODEL, FF_CHUNK), F32),
            pltpu.VMEM((2, D_MODEL, FF_CHUNK), F32),
            pltpu.VMEM((2, FF_CHUNK, D_MODEL), F32),
            pltpu.SemaphoreType.DMA((2, 3)),
        ],
        compiler_params=pltpu.CompilerParams(
            dimension_semantics=("arbitrary",), vmem_limit_bytes=VMEM_LIMIT_BYTES),
        name="ffn_ln",
    )(x2d, wg, wu, wd, ln_g, ln_b)


def _mixer_kernel(layer,
                  x_ref, win_ref, wout_ref, bucket_ref, relb_ref,
                  cw_ref, cb_ref, cg_ref, cbeta_ref,
                  lw_ref, lb_ref, wa_ref, ba_ref, wx_ref, bx_ref, lam_ref,
                  sink_ref, g_ref, b_ref,
                  o_ref,
                  bias_tab, chist, xhist, hcar, u, y, cs, xb, *kv):
    T = SEQ_TILE
    s_idx = pl.program_id(1)

    @pl.when((pl.program_id(0) == 0) & (s_idx == 0))
    def _build_bias_table():
        bucket = bucket_ref[...]
        for head in range(N_Q_HEADS):
            tab = jnp.zeros((BLOCK, 2 * BLOCK), F32)
            for bkt in range(REL_BUCKETS):
                tab = jnp.where(bucket == bkt, relb_ref[bkt, head], tab)
            bias_tab[head] = tab

    @pl.when(s_idx == 0)
    def _start_of_sequence():
        chist[...] = jnp.zeros((CONV_HIST, D_CONV), F32)
        xhist[...] = jnp.zeros((LRU_HIST, D_LRU), F32)
        hcar[...] = jnp.zeros((1, D_LRU), F32)
        for buf in kv:
            buf[0:BLOCK, :] = jnp.zeros((BLOCK, KV_DIM), BF16)

    o1 = 2 * D_CONV
    o2 = o1 + 2 * D_LRU
    o3 = o2 + D_ATTN
    n_rb = T // ROW_BLOCK

    def in_proj():
        x_bf = x_ref[0].astype(BF16)
        for c0, c1 in ((0, o1), (o1, o2), (o3, D_IN), (o2, o3)):
            u[:, c0:c1] = _dot(x_bf, win_ref[:, c0:c1])

    def conv_prep():
        yglu = u[:, 0:D_CONV] * jax.nn.sigmoid(u[:, D_CONV:2 * D_CONV])
        n_rows = CONV_HIST + T
        whole = jnp.concatenate([chist[...], yglu], axis=0)
        cs[0] = whole
        for r in range(1, SUBLANES):
            cs[r] = pltpu.roll(whole, n_rows - r, 0)
        chist[...] = yglu[T - CONV_HIST:T, :]

    def conv_block(rb):
        off = CONV_HIST - (CONV_KERNEL - 1)
        acc = jnp.broadcast_to(cb_ref[...], (ROW_BLOCK, D_CONV))
        for tap in range(CONV_KERNEL):
            r = (off + tap) % SUBLANES
            first = rb * ROW_BLOCK + off + tap - r
            acc = acc + cw_ref[tap:tap + 1, :] * cs[r, first:first + ROW_BLOCK, :]
        yc = _layernorm(acc, cg_ref[...], cbeta_ref[...])
        y[rb * ROW_BLOCK:(rb + 1) * ROW_BLOCK, 0:D_CONV] = (yc * jax.nn.sigmoid(yc)).astype(BF16)

    lru = {}

    def lru_prep():
        lam = lam_ref[...]
        lru["c_log_sig"] = LRU_C * -(jnp.maximum(-lam, 0.0) + jnp.log1p(jnp.exp(-jnp.abs(lam))))
        x_new = u[:, o1:o1 + D_LRU]
        xb[0:LRU_HIST, :] = xhist[...]
        xb[LRU_HIST:LRU_HIST + T, :] = x_new
        xhist[...] = x_new[T - LRU_HIST:T, :]
        lru["carry"] = hcar[...]

    def lru_block(rb):
        off = LRU_HIST - (LRU_CONV - 1)
        halo = ROW_BLOCK + SUBLANES
        n_grp = ROW_BLOCK // SUBLANES
        sub = lax.broadcasted_iota(jnp.int32, (n_grp, SUBLANES, D_LRU), 1)
        rows = slice(rb * ROW_BLOCK, (rb + 1) * ROW_BLOCK)
        blk = xb[rb * ROW_BLOCK:rb * ROW_BLOCK + halo, :]
        xc = jnp.broadcast_to(lb_ref[...], (ROW_BLOCK, D_LRU))
        for tap in range(LRU_CONV):
            r = (off + tap) % SUBLANES
            first = off + tap - r
            src = blk if r == 0 else pltpu.roll(blk, halo - r, 0)
            xc = xc + lw_ref[tap:tap + 1, :] * src[first:first + ROW_BLOCK, :]
        xcb = xc.astype(BF16)
        r_gate = jax.nn.sigmoid(_dot(xcb, wa_ref[...]) + ba_ref[...])
        i_gate = jax.nn.sigmoid(_dot(xcb, wx_ref[...]) + bx_ref[...])
        log_a = r_gate * lru["c_log_sig"]
        a = jnp.exp(log_a)
        mult = jnp.sqrt(-jnp.tanh(log_a) * (a * a + 1.0))
        bterm = mult * (i_gate * xc)
        a3 = a.reshape(n_grp, SUBLANES, D_LRU)
        b3 = bterm.reshape(n_grp, SUBLANES, D_LRU)
        k = 1
        while k < SUBLANES:
            a_prev = jnp.where(sub >= k, pltpu.roll(a3, k, 1), 1.0)
            b_prev = jnp.where(sub >= k, pltpu.roll(b3, k, 1), 0.0)
            b3 = a3 * b_prev + b3
            a3 = a3 * a_prev
            k *= 2
        carry = lru["carry"]
        h_groups = []
        for grp_i in range(n_grp):
            h_grp = a3[grp_i] * carry + b3[grp_i]
            carry = h_grp[SUBLANES - 1:SUBLANES, :]
            h_groups.append(h_grp)
        lru["carry"] = carry
        h = jnp.concatenate(h_groups, axis=0)
        gb = u[rows, o1 + D_LRU:o1 + 2 * D_LRU]
        y[rows, D_CONV:D_CONV + D_LRU] = (h * _gelu_tanh(gb)).astype(BF16)

    k_lo = kv[0:N_KV_HEADS]
    k_hi = kv[N_KV_HEADS:2 * N_KV_HEADS]
    v_lo = kv[2 * N_KV_HEADS:3 * N_KV_HEADS]
    v_hi = kv[3 * N_KV_HEADS:4 * N_KV_HEADS]
    heads_per_kv = N_Q_HEADS // N_KV_HEADS

    def kv_prep():
        kk = u[:, o3:o3 + KV_DIM] * (HEAD_DIM ** -0.5)
        vv = u[:, o3 + KV_DIM:o3 + 2 * KV_DIM]
        lo = lax.broadcasted_iota(jnp.int32, (T, LANES), 1) < HEAD_DIM
        k_rot = pltpu.roll(kk, HEAD_DIM, 1)
        v_rot = pltpu.roll(vv, HEAD_DIM, 1)
        new_rows = slice(BLOCK, BLOCK + T)
        for lo_bufs, hi_bufs, straight, rotated in ((k_lo, k_hi, kk, k_rot), (v_lo, v_hi, vv, v_rot)):
            lo_bufs[0][new_rows, :] = jnp.where(lo, straight, 0.0).astype(BF16)
            hi_bufs[0][new_rows, :] = jnp.where(lo, 0.0, rotated).astype(BF16)
            lo_bufs[1][new_rows, :] = jnp.where(lo, rotated, 0.0).astype(BF16)
            hi_bufs[1][new_rows, :] = jnp.where(lo, 0.0, straight).astype(BF16)

    def attn(qb, grp):
        qi = lax.broadcasted_iota(jnp.int32, (BLOCK, 2 * BLOCK), 0)
        kj = lax.broadcasted_iota(jnp.int32, (BLOCK, 2 * BLOCK), 1)
        dist = qi - kj + BLOCK
        valid = pltpu.bitcast(dist, jnp.uint32) < WINDOW
        if qb == 0:
            valid = valid & (kj >= jnp.where(s_idx == 0, BLOCK, 0))
        r0 = qb * BLOCK
        q_rows = slice(r0, r0 + BLOCK)
        band = slice(r0, r0 + 2 * BLOCK)
        chunks = range(grp * heads_per_kv // 2, (grp + 1) * heads_per_kv // 2)
        q_stack = jnp.concatenate(
            [u[q_rows, o2 + c * LANES:o2 + (c + 1) * LANES].astype(BF16) for c in chunks], axis=0)
        s_first = _dot_nt(q_stack, k_lo[grp][band, :])
        s_second = _dot_nt(q_stack, k_hi[grp][band, :])
        probs = []
        for hh in range(heads_per_kv):
            head = grp * heads_per_kv + hh
            pair_rows = slice((hh // 2) * BLOCK, (hh // 2 + 1) * BLOCK)
            s = (s_first if hh % 2 == 0 else s_second)[pair_rows, :] + bias_tab[head]
            s = jnp.where(valid, s, -1e30)
            sink = sink_ref[layer, head]
            m = jnp.maximum(jnp.max(s, axis=-1, keepdims=True), sink)
            e = jnp.exp(s - m)
            den = jnp.sum(e, axis=-1, keepdims=True) + jnp.exp(sink - m)
            probs.append((e / den).astype(BF16))
        p_rows = [jnp.concatenate(probs[2 * i:2 * i + 2], axis=1) for i in range(heads_per_kv // 2)]
        v_both = jnp.concatenate([v_lo[grp][band, :], v_hi[grp][band, :]], axis=0)
        o_pairs = _dot(jnp.concatenate(p_rows, axis=0), v_both)
        for i, c in enumerate(chunks):
            col = D_CONV + D_LRU + c * LANES
            y[q_rows, col:col + LANES] = o_pairs[i * BLOCK:(i + 1) * BLOCK, :].astype(BF16)

    def out_proj(qb):
        q_rows = slice(qb * BLOCK, (qb + 1) * BLOCK)
        hproj = _dot(y[q_rows, :], wout_ref[...])
        o_ref[0, q_rows, :] = _layernorm(ALPHA * x_ref[0, q_rows, :] + hproj, g_ref[...], b_ref[...])

    in_proj()
    conv_prep()
    for rb in range(n_rb):
        conv_block(rb)
    lru_prep()
    for rb in range(n_rb):
        lru_block(rb)
    kv_prep()
    for qb in range(T // BLOCK):
        for grp in range(N_KV_HEADS):
            attn(qb, grp)
        out_proj(qb)
    hcar[...] = lru["carry"]
    for buf in kv:
        buf[0:BLOCK, :] = buf[T:T + BLOCK, :]


def _mixer_call(x, layer, w_in, w_out, bucket, rel_bias, cw, cb, cg, cbeta, lw, lb, wa, ba, wx, bx, lam,
                sinks, ln_g, ln_b):
    B, S, _ = x.shape
    T = SEQ_TILE
    x_spec = pl.BlockSpec((1, T, D_MODEL), lambda bi, si: (bi, si, 0))
    lay = (layer,)
    smem = pl.BlockSpec(memory_space=pltpu.SMEM)
    in_specs = [
        x_spec,
        _resident((D_MODEL, D_IN), lay),
        _resident((D_MODEL, D_MODEL), lay),
        _resident((BLOCK, 2 * BLOCK)),
        smem,
        _resident((CONV_KERNEL, D_CONV), lay), _resident((1, D_CONV), lay),
        _resident((1, D_CONV), lay), _resident((1, D_CONV), lay),
        _resident((LRU_CONV, D_LRU), lay), _resident((1, D_LRU), lay),
        _resident((D_LRU, D_LRU), lay), _resident((1, D_LRU), lay),
        _resident((D_LRU, D_LRU), lay), _resident((1, D_LRU), lay),
        _resident((1, D_LRU), lay),
        smem,
        _resident((1, D_MODEL), (layer, 1)), _resident((1, D_MODEL), (layer, 1)),
    ]
    scratch = (
        [pltpu.VMEM((N_Q_HEADS, BLOCK, 2 * BLOCK), F32),
         pltpu.VMEM((CONV_HIST, D_CONV), F32),
         pltpu.VMEM((LRU_HIST, D_LRU), F32),
         pltpu.VMEM((1, D_LRU), F32),
         pltpu.VMEM((T, D_IN), F32),
         pltpu.VMEM((T, D_MODEL), BF16),
         pltpu.VMEM((SUBLANES, CONV_HIST + T, D_CONV), F32),
         pltpu.VMEM((LRU_HIST + T, D_LRU), F32)]
        + [pltpu.VMEM((BLOCK + T, KV_DIM), BF16) for _ in range(N_KV_BUFS)])
    return pl.pallas_call(
        functools.partial(_mixer_kernel, layer),
        grid=(B, S // T),
        in_specs=in_specs,
        out_specs=x_spec,
        out_shape=jax.ShapeDtypeStruct(x.shape, F32),
        scratch_shapes=scratch,
        compiler_params=pltpu.CompilerParams(
            dimension_semantics=("arbitrary", "arbitrary"), vmem_limit_bytes=VMEM_LIMIT_BYTES),
        name="mixer_ln",
    )(x, w_in, w_out, bucket, rel_bias, cw, cb, cg, cbeta, lw, lb, wa, ba, wx, bx, lam, sinks, ln_g, ln_b)


def _bucket_table():
    qi = np.arange(BLOCK)[:, None]
    kj = np.arange(2 * BLOCK)[None, :]
    dist = np.maximum(qi - kj + BLOCK, 0)
    max_exact = REL_BUCKETS // 2
    ratio = np.log(np.maximum(dist, 1).astype(np.float32) / np.float32(max_exact)) / np.float32(
        math.log(REL_MAX_DIST / max_exact))
    large = max_exact + (ratio * np.float32(REL_BUCKETS - max_exact)).astype(np.int32)
    return np.where(dist < max_exact, dist, np.minimum(large, REL_BUCKETS - 1)).astype(np.int32)


def _block_diag(w):
    l, h, d, _ = w.shape
    eye = jnp.eye(h, dtype=w.dtype)
    return (eye[None, :, None, :, None] * w[:, :, :, None, :]).reshape(l, h * d, h * d)


def kernel(x, rel_bias, ln_g, ln_b, ffn_w_gate, ffn_w_up, ffn_w_down, w_in, conv_dw_w, conv_dw_b, conv_ln_g, conv_ln_b, lru_conv_w, lru_conv_b, lru_wa, lru_ba, lru_wx, lru_bx, lru_lambda, attn_sinks, w_out):
    B, S, D = x.shape
    assert D == D_MODEL and S % SEQ_TILE == 0 and (B * S) % FFN_TILE == 0
    rows = lambda v: v.reshape(v.shape[:-1] + (1, v.shape[-1]))
    wg, wu, wd = ffn_w_gate, ffn_w_up, ffn_w_down
    w_in_b, w_out_b = w_in.astype(BF16), w_out.astype(BF16)
    wa, wx = _block_diag(lru_wa).astype(BF16), _block_diag(lru_wx).astype(BF16)
    g4, b4 = rows(ln_g), rows(ln_b)
    bucket = _bucket_table()
    for l in range(DEPTH):
        x = _ffn_call(x.reshape(B * S, D), wg, wu, wd, g4, b4, l, 0).reshape(B, S, D)
        x = _mixer_call(
            x, l, w_in_b, w_out_b, bucket, rel_bias,
            conv_dw_w, rows(conv_dw_b), rows(conv_ln_g), rows(conv_ln_b),
            lru_conv_w, rows(lru_conv_b), wa, rows(lru_ba), wx, rows(lru_bx), rows(lru_lambda),
            attn_sinks, g4, b4)
        x = _ffn_call(x.reshape(B * S, D), wg, wu, wd, g4, b4, l, 1).reshape(B, S, D)
    return x
```

```python
import functools
import math

import jax
import jax.numpy as jnp
import numpy as np
from jax import lax
from jax.experimental import pallas as pl
from jax.experimental.pallas import tpu as pltpu

D_MODEL = 1024
DEPTH = 2
D_CONV = 256
D_LRU = 256
D_ATTN = 512
CONV_KERNEL = 31
LRU_HEADS = 4
LRU_HEAD_DIM = 64
LRU_CONV = 4
LRU_C = 8.0
HEAD_DIM = 64
N_Q_HEADS = 8
N_KV_HEADS = 2
KV_DIM = 128
WINDOW = 128
BLOCK = 128
REL_BUCKETS = 32
REL_MAX_DIST = 128
D_FF = 2816
ALPHA = (2.0 * DEPTH) ** 0.25
LN_EPS = 1e-5
D_IN = 2 * D_CONV + 2 * D_LRU + D_ATTN + 2 * KV_DIM

LANES = 128
SUBLANES = 8
FF_CHUNK = 256
FFN_TILE = 512
SEQ_TILE = 512
ROW_BLOCK = 64
CONV_HIST = 32
LRU_HIST = 8
N_KV_BUFS = 6
VMEM_LIMIT_BYTES = 56 * 1024 * 1024

F32 = jnp.float32
BF16 = jnp.bfloat16


def _layernorm(z, g, b):
    mu = jnp.mean(z, axis=-1, keepdims=True)
    zc = z - mu
    var = jnp.mean(zc * zc, axis=-1, keepdims=True)
    return zc * lax.rsqrt(var + LN_EPS) * g + b


def _dot(a, b):
    return jnp.dot(a, b, preferred_element_type=F32)


def _dot_nt(a, b):
    return lax.dot_general(a, b, (((1,), (1,)), ((), ())), preferred_element_type=F32)


def _resident(shape, lead=()):
    block = (None,) * len(lead) + tuple(shape)
    index = tuple(lead) + (0,) * len(shape)
    return pl.BlockSpec(block, lambda *_: index, pipeline_mode=pl.Buffered(1))


def _zero_from(*values):
    word = None
    for v in values:
        bits = pltpu.bitcast(v, jnp.uint32)
        for grp in range(v.shape[0] // SUBLANES):
            for blk in range(v.shape[1] // LANES):
                piece = bits[grp * SUBLANES:(grp + 1) * SUBLANES, blk * LANES:(blk + 1) * LANES]
                word = piece if word is None else word | piece
    zero = lax.shift_right_logical(lax.shift_right_logical(word, jnp.uint32(16)), jnp.uint32(16))
    return pltpu.bitcast(zero, F32)[0:1, 0:1]


def _ffn_kernel(n_tiles, layer, half, norm_input, *refs):
    if norm_input:
        (x0_ref, xnext_ref, wg_hbm, wu_hbm, wd_hbm, ln_ref, o_ref,
         h_ref, z_ref, wg_ref, wu_ref, wd_ref, stage_g, stage_u, stage_d, sems, xn_ref, xn_next_ref) = refs
    else:
        (xn_ref, wg_hbm, wu_hbm, wd_hbm, ln_ref, o_ref,
         h_ref, z_ref, wg_ref, wu_ref, wd_ref, stage_g, stage_u, stage_d, sems) = refs
    i = pl.program_id(0)
    n_chunks = D_FF // FF_CHUNK
    k_out = 2 * half

    def normalise_previous():
        out = _layernorm(z_ref[...], ln_ref[k_out, 0:1, :], ln_ref[k_out, 1:2, :])
        o_ref[...] = out
        return out

    def normalise_input(src_ref):
        return _layernorm(src_ref[...], ln_ref[1, 0:1, :], ln_ref[1, 1:2, :])

    def swiglu_chunk(xb, c, extra=None):
        sl = slice(c * FF_CHUNK, (c + 1) * FF_CHUNK)
        gate = _dot(xb, wg_ref[:, sl])
        up = _dot(xb, wu_ref[:, sl])
        if extra is not None:
            up = up + extra
        h_ref[:, sl] = (gate * jax.nn.sigmoid(gate) * up).astype(BF16)

    def down_and_residual():
        y = _dot(h_ref[...], wd_ref[...])
        z_ref[...] = ALPHA * xn_ref[...] + 0.5 * y

    def weight_copies(c, slot):
        cols = pl.ds(c * FF_CHUNK, FF_CHUNK)
        return (
            pltpu.make_async_copy(wg_hbm.at[layer, half, :, cols], stage_g.at[slot], sems.at[slot, 0]),
            pltpu.make_async_copy(wu_hbm.at[layer, half, :, cols], stage_u.at[slot], sems.at[slot, 1]),
            pltpu.make_async_copy(wd_hbm.at[layer, half, cols, :], stage_d.at[slot], sems.at[slot, 2]),
        )

    @pl.when(i == 0)
    def _():
        for cp in weight_copies(0, 0):
            cp.start()
        if norm_input:
            xn_ref[...] = normalise_input(x0_ref)
            xn_next_ref[...] = normalise_input(xnext_ref)
        xb = xn_ref[...].astype(BF16)
        for c in range(n_chunks):
            slot = c % 2
            if c + 1 < n_chunks:
                for cp in weight_copies(c + 1, 1 - slot):
                    cp.start()
            for cp in weight_copies(c, slot):
                cp.wait()
            sl = slice(c * FF_CHUNK, (c + 1) * FF_CHUNK)
            wg_ref[:, sl] = stage_g[slot].astype(BF16)
            wu_ref[:, sl] = stage_u[slot].astype(BF16)
            wd_ref[sl, :] = stage_d[slot].astype(BF16)
            swiglu_chunk(xb, c)
        down_and_residual()

    @pl.when((i > 0) & (i < n_tiles))
    def _():
        done = [normalise_previous()]
        if norm_input:
            xn_ref[...] = xn_next_ref[...]
            nxt = normalise_input(xnext_ref)
            xn_next_ref[...] = nxt
            done.append(nxt)
        zero_rows = jnp.broadcast_to(_zero_from(*done), (FFN_TILE, FF_CHUNK))
        xb = xn_ref[...].astype(BF16)
        for c in range(n_chunks):
            swiglu_chunk(xb, c, zero_rows if c == n_chunks - 1 else None)
        down_and_residual()

    @pl.when(i == n_tiles)
    def _():
        normalise_previous()


def _ffn_call(x2d, wg, wu, wd, ln, layer, half, norm_input):
    n_tok = x2d.shape[0]
    n_tiles = n_tok // FFN_TILE
    hbm = pl.BlockSpec(memory_space=pl.ANY)
    tile = (FFN_TILE, D_MODEL)
    if norm_input:
        x_specs = [pl.BlockSpec(tile, lambda i: (0, 0)),
                   pl.BlockSpec(tile, lambda i: (jnp.minimum(i + 1, n_tiles - 1), 0))]
        x_args = [x2d, x2d]
        x_scratch = [pltpu.VMEM(tile, F32), pltpu.VMEM(tile, F32)]
    else:
        x_specs = [pl.BlockSpec(tile, lambda i: (jnp.minimum(i, n_tiles - 1), 0))]
        x_args = [x2d]
        x_scratch = []
    return pl.pallas_call(
        functools.partial(_ffn_kernel, n_tiles, layer, half, norm_input),
        grid=(n_tiles + 1,),
        in_specs=x_specs + [hbm, hbm, hbm, _resident((3, 2, D_MODEL), (layer,))],
        out_specs=pl.BlockSpec(tile, lambda i: (jnp.maximum(i - 1, 0), 0)),
        out_shape=jax.ShapeDtypeStruct((n_tok, D_MODEL), F32),
        scratch_shapes=[
            pltpu.VMEM((FFN_TILE, D_FF), BF16),
            pltpu.VMEM(tile, F32),
            pltpu.VMEM((D_MODEL, D_FF), BF16),
            pltpu.VMEM((D_MODEL, D_FF), BF16),
            pltpu.VMEM((D_FF, D_MODEL), BF16),
            pltpu.VMEM((2, D_MODEL, FF_CHUNK), F32),
            pltpu.VMEM((2, D_MODEL, FF_CHUNK), F32),
            pltpu.VMEM((2, FF_CHUNK, D_MODEL), F32),
            pltpu.SemaphoreType.DMA((2, 3)),
        ] + x_scratch,
        compiler_params=pltpu.CompilerParams(
            dimension_semantics=("arbitrary",), vmem_limit_bytes=VMEM_LIMIT_BYTES),
        name="ffn_ln",
    )(*x_args, wg, wu, wd, ln)


def _mixer_kernel(layer,
                  x_ref, win_ref, wout_ref, bucket_ref, relb_ref,
                  cw_ref, cb_ref, cg_ref, cbeta_ref,
                  lw_ref, lb_ref, wa_ref, ba_ref, wx_ref, bx_ref, lam_ref,
                  sink_ref,
                  o_ref,
                  bias_tab, chist, xhist, hcar, u, y, cs, xb, *kv):
    T = SEQ_TILE
    s_idx = pl.program_id(1)

    @pl.when((pl.program_id(0) == 0) & (s_idx == 0))
    def _build_bias_table():
        bucket = bucket_ref[...]
        for head in range(N_Q_HEADS):
            tab = jnp.zeros((BLOCK, 2 * BLOCK), F32)
            for bkt in range(REL_BUCKETS):
                tab = jnp.where(bucket == bkt, relb_ref[bkt, head], tab)
            bias_tab[head] = tab

    @pl.when(s_idx == 0)
    def _start_of_sequence():
        chist[...] = jnp.zeros((CONV_HIST, D_CONV), F32)
        xhist[...] = jnp.zeros((LRU_HIST, D_LRU), F32)
        hcar[...] = jnp.zeros((1, D_LRU), F32)
        for buf in kv:
            buf[0:BLOCK, :] = jnp.zeros((BLOCK, KV_DIM), BF16)

    o1 = 2 * D_CONV
    o2 = o1 + 2 * D_LRU
    o3 = o2 + D_ATTN
    n_rb = T // ROW_BLOCK

    def in_proj():
        x_bf = x_ref[0].astype(BF16)
        for c0, c1 in ((0, o1), (o1, o2), (o3, D_IN), (o2, o3)):
            u[:, c0:c1] = _dot(x_bf, win_ref[:, c0:c1])

    def conv_prep():
        yglu = u[:, 0:D_CONV] * jax.nn.sigmoid(u[:, D_CONV:2 * D_CONV])
        n_rows = CONV_HIST + T
        whole = jnp.concatenate([chist[...], yglu], axis=0)
        cs[0] = whole
        for r in range(1, SUBLANES):
            cs[r] = pltpu.roll(whole, n_rows - r, 0)
        chist[...] = yglu[T - CONV_HIST:T, :]

    def conv_block(rb):
        off = CONV_HIST - (CONV_KERNEL - 1)
        acc = jnp.broadcast_to(cb_ref[...], (ROW_BLOCK, D_CONV))
        for tap in range(CONV_KERNEL):
            r = (off + tap) % SUBLANES
            first = rb * ROW_BLOCK + off + tap - r
            acc = acc + cw_ref[tap:tap + 1, :] * cs[r, first:first + ROW_BLOCK, :]
        yc = _layernorm(acc, cg_ref[...], cbeta_ref[...])
        y[rb * ROW_BLOCK:(rb + 1) * ROW_BLOCK, 0:D_CONV] = (yc * jax.nn.sigmoid(yc)).astype(BF16)

    lru = {}

    def lru_prep():
        lam = lam_ref[...]
        lru["log_sig"] = -(jnp.maximum(-lam, 0.0) + jnp.log1p(jnp.exp(-jnp.abs(lam))))
        x_new = u[:, o1:o1 + D_LRU]
        xb[0:LRU_HIST, :] = xhist[...]
        xb[LRU_HIST:LRU_HIST + T, :] = x_new
        xhist[...] = x_new[T - LRU_HIST:T, :]
        lru["carry"] = hcar[...]

    def lru_block(rb):
        off = LRU_HIST - (LRU_CONV - 1)
        halo = ROW_BLOCK + SUBLANES
        n_grp = ROW_BLOCK // SUBLANES
        sub = lax.broadcasted_iota(jnp.int32, (n_grp, SUBLANES, D_LRU), 1)
        rows = slice(rb * ROW_BLOCK, (rb + 1) * ROW_BLOCK)
        blk = xb[rb * ROW_BLOCK:rb * ROW_BLOCK + halo, :]
        xc = jnp.broadcast_to(lb_ref[...], (ROW_BLOCK, D_LRU))
        for tap in range(LRU_CONV):
            r = (off + tap) % SUBLANES
            first = off + tap - r
            src = blk if r == 0 else pltpu.roll(blk, halo - r, 0)
            xc = xc + lw_ref[tap:tap + 1, :] * src[first:first + ROW_BLOCK, :]
        xcb = xc.astype(BF16)
        r_gate = jax.nn.sigmoid(_dot(xcb, wa_ref[...]) + ba_ref[...])
        i_gate = jax.nn.sigmoid(_dot(xcb, wx_ref[...]) + bx_ref[...])
        log_a = LRU_C * r_gate * lru["log_sig"]
        a = jnp.exp(log_a)
        mult = jnp.sqrt(-jnp.tanh(log_a) * (a * a + 1.0))
        bterm = mult * (i_gate * xc)
        a3 = a.reshape(n_grp, SUBLANES, D_LRU)
        b3 = bterm.reshape(n_grp, SUBLANES, D_LRU)
        k = 1
        while k < SUBLANES:
            a_prev = jnp.where(sub >= k, pltpu.roll(a3, k, 1), 1.0)
            b_prev = jnp.where(sub >= k, pltpu.roll(b3, k, 1), 0.0)
            b3 = a3 * b_prev + b3
            a3 = a3 * a_prev
            k *= 2
        carry = lru["carry"]
        h_groups = []
        for grp_i in range(n_grp):
            h_grp = a3[grp_i] * carry + b3[grp_i]
            carry = h_grp[SUBLANES - 1:SUBLANES, :]
            h_groups.append(h_grp)
        lru["carry"] = carry
        h = jnp.concatenate(h_groups, axis=0)
        gb = u[rows, o1 + D_LRU:o1 + 2 * D_LRU]
        y[rows, D_CONV:D_CONV + D_LRU] = (h * jax.nn.gelu(gb)).astype(BF16)

    k_dup = kv[0:N_KV_HEADS]
    v_lo = kv[N_KV_HEADS:2 * N_KV_HEADS]
    v_hi = kv[2 * N_KV_HEADS:3 * N_KV_HEADS]
    heads_per_kv = N_Q_HEADS // N_KV_HEADS

    def kv_prep():
        kk = u[:, o3:o3 + KV_DIM]
        vv = u[:, o3 + KV_DIM:o3 + 2 * KV_DIM]
        lo = lax.broadcasted_iota(jnp.int32, (T, LANES), 1) < HEAD_DIM
        k_rot = pltpu.roll(kk, HEAD_DIM, 1)
        v_rot = pltpu.roll(vv, HEAD_DIM, 1)
        new_rows = slice(BLOCK, BLOCK + T)
        k_dup[0][new_rows, :] = jnp.where(lo, kk, k_rot).astype(BF16)
        k_dup[1][new_rows, :] = jnp.where(lo, k_rot, kk).astype(BF16)
        v_lo[0][new_rows, :] = jnp.where(lo, vv, 0.0).astype(BF16)
        v_hi[0][new_rows, :] = jnp.where(lo, 0.0, v_rot).astype(BF16)
        v_lo[1][new_rows, :] = jnp.where(lo, v_rot, 0.0).astype(BF16)
        v_hi[1][new_rows, :] = jnp.where(lo, 0.0, vv).astype(BF16)

    def attn(qb, grp):
        qi = lax.broadcasted_iota(jnp.int32, (BLOCK, 2 * BLOCK), 0)
        kj = lax.broadcasted_iota(jnp.int32, (BLOCK, 2 * BLOCK), 1)
        dist = qi - kj + BLOCK
        valid = pltpu.bitcast(dist, jnp.uint32) < WINDOW
        if qb == 0:
            valid = valid & (kj >= jnp.where(s_idx == 0, BLOCK, 0))
        qlane_lo = lax.broadcasted_iota(jnp.int32, (BLOCK, LANES), 1) < HEAD_DIM
        r0 = qb * BLOCK
        q_rows = slice(r0, r0 + BLOCK)
        band = slice(r0, r0 + 2 * BLOCK)
        chunks = range(grp * heads_per_kv // 2, (grp + 1) * heads_per_kv // 2)
        q_parts = []
        for c in chunks:
            qc = u[q_rows, o2 + c * LANES:o2 + (c + 1) * LANES] * (HEAD_DIM ** -0.5)
            q_parts += [jnp.where(qlane_lo, qc, 0.0).astype(BF16), jnp.where(qlane_lo, 0.0, qc).astype(BF16)]
        s_all = _dot_nt(jnp.concatenate(q_parts, axis=0), k_dup[grp][band, :])
        probs = []
        for hh in range(heads_per_kv):
            head = grp * heads_per_kv + hh
            s = s_all[hh * BLOCK:(hh + 1) * BLOCK, :] + bias_tab[head]
            s = jnp.where(valid, s, -1e30)
            sink = sink_ref[layer, head]
            m = jnp.maximum(jnp.max(s, axis=-1, keepdims=True), sink)
            e = jnp.exp(s - m)
            den = jnp.sum(e, axis=-1, keepdims=True) + jnp.exp(sink - m)
            probs.append((e / den).astype(BF16))
        p_rows = [jnp.concatenate(probs[2 * i:2 * i + 2], axis=1) for i in range(heads_per_kv // 2)]
        v_both = jnp.concatenate([v_lo[grp][band, :], v_hi[grp][band, :]], axis=0)
        o_pairs = _dot(jnp.concatenate(p_rows, axis=0), v_both)
        for i, c in enumerate(chunks):
            col = D_CONV + D_LRU + c * LANES
            y[q_rows, col:col + LANES] = o_pairs[i * BLOCK:(i + 1) * BLOCK, :].astype(BF16)

    def out_proj(qb):
        q_rows = slice(qb * BLOCK, (qb + 1) * BLOCK)
        hproj = _dot(y[q_rows, :], wout_ref[...])
        o_ref[0, q_rows, :] = ALPHA * x_ref[0, q_rows, :] + hproj

    in_proj()
    conv_prep()
    for rb in range(n_rb):
        conv_block(rb)
    lru_prep()
    for rb in range(n_rb):
        lru_block(rb)
    kv_prep()
    for qb in range(T // BLOCK):
        for grp in range(N_KV_HEADS):
            attn(qb, grp)
        out_proj(qb)
    hcar[...] = lru["carry"]
    for buf in kv:
        buf[0:BLOCK, :] = buf[T:T + BLOCK, :]


def _mixer_call(x, layer, w_in, w_out, bucket, rel_bias, cw, cb, cg, cbeta, lw, lb, wa, ba, wx, bx, lam, sinks):
    B, S, _ = x.shape
    T = SEQ_TILE
    x_spec = pl.BlockSpec((1, T, D_MODEL), lambda bi, si: (bi, si, 0))
    lay = (layer,)
    smem = pl.BlockSpec(memory_space=pltpu.SMEM)
    in_specs = [
        x_spec,
        _resident((D_MODEL, D_IN), lay),
        _resident((D_MODEL, D_MODEL), lay),
        _resident((BLOCK, 2 * BLOCK)),
        smem,
        _resident((CONV_KERNEL, D_CONV), lay), _resident((1, D_CONV), lay),
        _resident((1, D_CONV), lay), _resident((1, D_CONV), lay),
        _resident((LRU_CONV, D_LRU), lay), _resident((1, D_LRU), lay),
        _resident((D_LRU, D_LRU), lay), _resident((1, D_LRU), lay),
        _resident((D_LRU, D_LRU), lay), _resident((1, D_LRU), lay),
        _resident((1, D_LRU), lay),
        smem,
    ]
    scratch = (
        [pltpu.VMEM((N_Q_HEADS, BLOCK, 2 * BLOCK), F32),
         pltpu.VMEM((CONV_HIST, D_CONV), F32),
         pltpu.VMEM((LRU_HIST, D_LRU), F32),
         pltpu.VMEM((1, D_LRU), F32),
         pltpu.VMEM((T, D_IN), F32),
         pltpu.VMEM((T, D_MODEL), BF16),
         pltpu.VMEM((SUBLANES, CONV_HIST + T, D_CONV), F32),
         pltpu.VMEM((LRU_HIST + T, D_LRU), F32)]
        + [pltpu.VMEM((BLOCK + T, KV_DIM), BF16) for _ in range(N_KV_BUFS)])
    return pl.pallas_call(
        functools.partial(_mixer_kernel, layer),
        grid=(B, S // T),
        in_specs=in_specs,
        out_specs=x_spec,
        out_shape=jax.ShapeDtypeStruct(x.shape, F32),
        scratch_shapes=scratch,
        compiler_params=pltpu.CompilerParams(
            dimension_semantics=("arbitrary", "arbitrary"), vmem_limit_bytes=VMEM_LIMIT_BYTES),
        name="mixer_ln",
    )(x, w_in, w_out, bucket, rel_bias, cw, cb, cg, cbeta, lw, lb, wa, ba, wx, bx, lam, sinks)


def _bucket_table():
    qi = np.arange(BLOCK)[:, None]
    kj = np.arange(2 * BLOCK)[None, :]
    dist = np.maximum(qi - kj + BLOCK, 0)
    max_exact = REL_BUCKETS // 2
    ratio = np.log(np.maximum(dist, 1).astype(np.float32) / np.float32(max_exact)) / np.float32(
        math.log(REL_MAX_DIST / max_exact))
    large = max_exact + (ratio * np.float32(REL_BUCKETS - max_exact)).astype(np.int32)
    return np.where(dist < max_exact, dist, np.minimum(large, REL_BUCKETS - 1)).astype(np.int32)


def _block_diag(w):
    l, h, d, _ = w.shape
    eye = jnp.eye(h, dtype=w.dtype)
    return (eye[None, :, None, :, None] * w[:, :, :, None, :]).reshape(l, h * d, h * d)


def kernel(x, rel_bias, ln_g, ln_b, ffn_w_gate, ffn_w_up, ffn_w_down, w_in, conv_dw_w, conv_dw_b, conv_ln_g, conv_ln_b, lru_conv_w, lru_conv_b, lru_wa, lru_ba, lru_wx, lru_bx, lru_lambda, attn_sinks, w_out):
    B, S, D = x.shape
    assert D == D_MODEL and S % SEQ_TILE == 0 and (B * S) % FFN_TILE == 0
    rows = lambda v: v.reshape(v.shape[:-1] + (1, v.shape[-1]))
    wg, wu, wd = ffn_w_gate, ffn_w_up, ffn_w_down
    w_in_b, w_out_b = w_in.astype(BF16), w_out.astype(BF16)
    wa, wx = _block_diag(lru_wa).astype(BF16), _block_diag(lru_wx).astype(BF16)
    ln = jnp.stack([ln_g, ln_b], axis=2)
    bucket = _bucket_table()
    for l in range(DEPTH):
        x = _ffn_call(x.reshape(B * S, D), wg, wu, wd, ln, l, 0, norm_input=False).reshape(B, S, D)
        z = _mixer_call(
            x, l, w_in_b, w_out_b, bucket, rel_bias,
            conv_dw_w, rows(conv_dw_b), rows(conv_ln_g), rows(conv_ln_b),
            lru_conv_w, rows(lru_conv_b), wa, rows(lru_ba), wx, rows(lru_bx), rows(lru_lambda),
            attn_sinks)
        x = _ffn_call(z.reshape(B * S, D), wg, wu, wd, ln, l, 1, norm_input=True).reshape(B, S, D)
    return x
```

```python
import functools
import math

import jax
import jax.numpy as jnp
import numpy as np
from jax import lax
from jax.experimental import pallas as pl
from jax.experimental.pallas import tpu as pltpu

D_MODEL = 1024
DEPTH = 2
D_CONV = 256
D_LRU = 256
D_ATTN = 512
CONV_KERNEL = 31
LRU_HEADS = 4
LRU_HEAD_DIM = 64
LRU_CONV = 4
LRU_C = 8.0
HEAD_DIM = 64
N_Q_HEADS = 8
N_KV_HEADS = 2
KV_DIM = 128
WINDOW = 128
BLOCK = 128
REL_BUCKETS = 32
REL_MAX_DIST = 128
D_FF = 2816
ALPHA = (2.0 * DEPTH) ** 0.25
LN_EPS = 1e-5
D_IN = 2 * D_CONV + 2 * D_LRU + D_ATTN + 2 * KV_DIM

LANES = 128
SUBLANES = 8
FF_CHUNK = 256
FFN_TILE = 512
SEQ_TILE = 512
CONV_ROWS = 128
LRU_ROWS = 128
W_SLOTS = 3
CONV_HIST = 32
LRU_HIST = 8
N_KV_BUFS = 6
VMEM_LIMIT_BYTES = 56 * 1024 * 1024

F32 = jnp.float32
BF16 = jnp.bfloat16


def _layernorm(z, g, b):
    mu = jnp.mean(z, axis=-1, keepdims=True)
    zc = z - mu
    var = jnp.mean(zc * zc, axis=-1, keepdims=True)
    return zc * lax.rsqrt(var + LN_EPS) * g + b


def _dot(a, b):
    return jnp.dot(a, b, preferred_element_type=F32)


def _dot_nt(a, b):
    return lax.dot_general(a, b, (((1,), (1,)), ((), ())), preferred_element_type=F32)


def _resident(shape, lead=()):
    block = (None,) * len(lead) + tuple(shape)
    index = tuple(lead) + (0,) * len(shape)
    return pl.BlockSpec(block, lambda *_: index, pipeline_mode=pl.Buffered(1))


def _zero_from(*values):
    word = None
    for v in values:
        bits = pltpu.bitcast(v, jnp.uint32)
        for grp in range(v.shape[0] // SUBLANES):
            for blk in range(v.shape[1] // LANES):
                piece = bits[grp * SUBLANES:(grp + 1) * SUBLANES, blk * LANES:(blk + 1) * LANES]
                word = piece if word is None else word | piece
    zero = lax.shift_right_logical(lax.shift_right_logical(word, jnp.uint32(16)), jnp.uint32(16))
    return pltpu.bitcast(zero, F32)[0:1, 0:1]


def _ffn_kernel(n_tiles, layer, half, norm_input, *refs):
    if norm_input:
        (x0_ref, xnext_ref, wg_hbm, wu_hbm, wd_hbm, ln_ref, o_ref,
         h_ref, z_ref, wg_ref, wu_ref, wd_ref, stage_g, stage_u, stage_d, sems, xn_ref, xn_next_ref) = refs
    else:
        (xn_ref, wg_hbm, wu_hbm, wd_hbm, ln_ref, o_ref,
         h_ref, z_ref, wg_ref, wu_ref, wd_ref, stage_g, stage_u, stage_d, sems) = refs
    i = pl.program_id(0)
    n_chunks = D_FF // FF_CHUNK
    k_out = 2 * half

    def normalise_previous():
        out = _layernorm(z_ref[...], ln_ref[k_out, 0:1, :], ln_ref[k_out, 1:2, :])
        o_ref[...] = out
        return out

    def normalise_input(src_ref):
        return _layernorm(src_ref[...], ln_ref[1, 0:1, :], ln_ref[1, 1:2, :])

    def swiglu_chunk(xb, c, extra=None):
        sl = slice(c * FF_CHUNK, (c + 1) * FF_CHUNK)
        gate = _dot(xb, wg_ref[:, sl])
        up = _dot(xb, wu_ref[:, sl])
        if extra is not None:
            up = up + extra
        h_ref[:, sl] = (gate * jax.nn.sigmoid(gate) * up).astype(BF16)

    def down_and_residual():
        y = _dot(h_ref[...], wd_ref[...])
        z_ref[...] = ALPHA * xn_ref[...] + 0.5 * y

    def weight_copies(c, slot):
        cols = pl.ds(c * FF_CHUNK, FF_CHUNK)
        return (
            pltpu.make_async_copy(wg_hbm.at[layer, half, :, cols], stage_g.at[slot], sems.at[slot, 0]),
            pltpu.make_async_copy(wu_hbm.at[layer, half, :, cols], stage_u.at[slot], sems.at[slot, 1]),
            pltpu.make_async_copy(wd_hbm.at[layer, half, cols, :], stage_d.at[slot], sems.at[slot, 2]),
        )

    @pl.when(i == 0)
    def _():
        for c in range(W_SLOTS - 1):
            for cp in weight_copies(c, c):
                cp.start()
        if norm_input:
            xn_ref[...] = normalise_input(x0_ref)
            xn_next_ref[...] = normalise_input(xnext_ref)
        xb = xn_ref[...].astype(BF16)
        for c in range(n_chunks):
            slot = c % W_SLOTS
            ahead = c + W_SLOTS - 1
            if ahead < n_chunks:
                for cp in weight_copies(ahead, ahead % W_SLOTS):
                    cp.start()
            for cp in weight_copies(c, slot):
                cp.wait()
            sl = slice(c * FF_CHUNK, (c + 1) * FF_CHUNK)
            wg_ref[:, sl] = stage_g[slot].astype(BF16)
            wu_ref[:, sl] = stage_u[slot].astype(BF16)
            wd_ref[sl, :] = stage_d[slot].astype(BF16)
            swiglu_chunk(xb, c)
        down_and_residual()

    @pl.when((i > 0) & (i < n_tiles))
    def _():
        done = [normalise_previous()]
        if norm_input:
            xn_ref[...] = xn_next_ref[...]
            nxt = normalise_input(xnext_ref)
            xn_next_ref[...] = nxt
            done.append(nxt)
        zero_rows = jnp.broadcast_to(_zero_from(*done), (FFN_TILE, FF_CHUNK))
        xb = xn_ref[...].astype(BF16)
        for c in range(n_chunks):
            swiglu_chunk(xb, c, zero_rows if c == n_chunks - 1 else None)
        down_and_residual()

    @pl.when(i == n_tiles)
    def _():
        normalise_previous()


def _ffn_call(x2d, wg, wu, wd, ln, layer, half, norm_input):
    n_tok = x2d.shape[0]
    n_tiles = n_tok // FFN_TILE
    hbm = pl.BlockSpec(memory_space=pl.ANY)
    tile = (FFN_TILE, D_MODEL)
    if norm_input:
        x_specs = [pl.BlockSpec(tile, lambda i: (0, 0)),
                   pl.BlockSpec(tile, lambda i: (jnp.minimum(i + 1, n_tiles - 1), 0))]
        x_args = [x2d, x2d]
        x_scratch = [pltpu.VMEM(tile, F32), pltpu.VMEM(tile, F32)]
    else:
        x_specs = [pl.BlockSpec(tile, lambda i: (jnp.minimum(i, n_tiles - 1), 0))]
        x_args = [x2d]
        x_scratch = []
    return pl.pallas_call(
        functools.partial(_ffn_kernel, n_tiles, layer, half, norm_input),
        grid=(n_tiles + 1,),
        in_specs=x_specs + [hbm, hbm, hbm, _resident((3, 2, D_MODEL), (layer,))],
        out_specs=pl.BlockSpec(tile, lambda i: (jnp.maximum(i - 1, 0), 0)),
        out_shape=jax.ShapeDtypeStruct((n_tok, D_MODEL), F32),
        scratch_shapes=[
            pltpu.VMEM((FFN_TILE, D_FF), BF16),
            pltpu.VMEM(tile, F32),
            pltpu.VMEM((D_MODEL, D_FF), BF16),
            pltpu.VMEM((D_MODEL, D_FF), BF16),
            pltpu.VMEM((D_FF, D_MODEL), BF16),
            pltpu.VMEM((W_SLOTS, D_MODEL, FF_CHUNK), F32),
            pltpu.VMEM((W_SLOTS, D_MODEL, FF_CHUNK), F32),
            pltpu.VMEM((W_SLOTS, FF_CHUNK, D_MODEL), F32),
            pltpu.SemaphoreType.DMA((W_SLOTS, 3)),
        ] + x_scratch,
        compiler_params=pltpu.CompilerParams(
            dimension_semantics=("arbitrary",), vmem_limit_bytes=VMEM_LIMIT_BYTES),
        name="ffn_ln",
    )(*x_args, wg, wu, wd, ln)


def _mixer_kernel(layer,
                  x_ref, win_ref, wout_ref, bucket_ref, relb_ref,
                  cw_ref, cb_ref, cg_ref, cbeta_ref,
                  lw_ref, lb_ref, wa_ref, ba_ref, wx_ref, bx_ref, lam_ref,
                  sink_ref,
                  o_ref,
                  bias_tab, chist, xhist, hcar, u, y, cs, xb, *kv):
    T = SEQ_TILE
    s_idx = pl.program_id(1)

    @pl.when((pl.program_id(0) == 0) & (s_idx == 0))
    def _build_bias_table():
        bucket = bucket_ref[...]
        for head in range(N_Q_HEADS):
            tab = jnp.zeros((BLOCK, 2 * BLOCK), F32)
            for bkt in range(REL_BUCKETS):
                tab = jnp.where(bucket == bkt, relb_ref[bkt, head], tab)
            bias_tab[head] = tab

    @pl.when(s_idx == 0)
    def _start_of_sequence():
        chist[...] = jnp.zeros((CONV_HIST, D_CONV), F32)
        xhist[...] = jnp.zeros((LRU_HIST, D_LRU), F32)
        hcar[...] = jnp.zeros((1, D_LRU), F32)
        for buf in kv:
            buf[0:BLOCK, :] = jnp.zeros((BLOCK, KV_DIM), BF16)

    o1 = 2 * D_CONV
    o2 = o1 + 2 * D_LRU
    o3 = o2 + D_ATTN

    def in_proj():
        x_bf = x_ref[0].astype(BF16)
        for c0, c1 in ((0, o1), (o1, o2), (o3, D_IN), (o2, o3)):
            u[:, c0:c1] = _dot(x_bf, win_ref[:, c0:c1])

    def conv_prep():
        yglu = u[:, 0:D_CONV] * jax.nn.sigmoid(u[:, D_CONV:2 * D_CONV])
        n_rows = CONV_HIST + T
        whole = jnp.concatenate([chist[...], yglu], axis=0)
        cs[0] = whole
        for r in range(1, SUBLANES):
            cs[r] = pltpu.roll(whole, n_rows - r, 0)
        chist[...] = yglu[T - CONV_HIST:T, :]

    def conv_block(rb):
        off = CONV_HIST - (CONV_KERNEL - 1)
        acc = jnp.broadcast_to(cb_ref[...], (CONV_ROWS, D_CONV))
        for tap in range(CONV_KERNEL):
            r = (off + tap) % SUBLANES
            first = rb * CONV_ROWS + off + tap - r
            acc = acc + cw_ref[tap:tap + 1, :] * cs[r, first:first + CONV_ROWS, :]
        yc = _layernorm(acc, cg_ref[...], cbeta_ref[...])
        y[rb * CONV_ROWS:(rb + 1) * CONV_ROWS, 0:D_CONV] = (yc * jax.nn.sigmoid(yc)).astype(BF16)

    lru = {}

    def lru_prep():
        lam = lam_ref[...]
        lru["log_sig"] = -(jnp.maximum(-lam, 0.0) + jnp.log1p(jnp.exp(-jnp.abs(lam))))
        x_new = u[:, o1:o1 + D_LRU]
        xb[0:LRU_HIST, :] = xhist[...]
        xb[LRU_HIST:LRU_HIST + T, :] = x_new
        xhist[...] = x_new[T - LRU_HIST:T, :]
        lru["carry"] = hcar[...]

    def lru_block(rb):
        off = LRU_HIST - (LRU_CONV - 1)
        halo = LRU_ROWS + SUBLANES
        n_grp = LRU_ROWS // SUBLANES
        sub = lax.broadcasted_iota(jnp.int32, (n_grp, SUBLANES, D_LRU), 1)
        rows = slice(rb * LRU_ROWS, (rb + 1) * LRU_ROWS)
        blk = xb[rb * LRU_ROWS:rb * LRU_ROWS + halo, :]
        xc = jnp.broadcast_to(lb_ref[...], (LRU_ROWS, D_LRU))
        for tap in range(LRU_CONV):
            r = (off + tap) % SUBLANES
            first = off + tap - r
            src = blk if r == 0 else pltpu.roll(blk, halo - r, 0)
            xc = xc + lw_ref[tap:tap + 1, :] * src[first:first + LRU_ROWS, :]
        xcb = xc.astype(BF16)
        r_gate = jax.nn.sigmoid(_dot(xcb, wa_ref[...]) + ba_ref[...])
        i_gate = jax.nn.sigmoid(_dot(xcb, wx_ref[...]) + bx_ref[...])
        log_a = LRU_C * r_gate * lru["log_sig"]
        a = jnp.exp(log_a)
        mult = jnp.sqrt(-jnp.tanh(log_a) * (a * a + 1.0))
        bterm = mult * (i_gate * xc)
        a3 = a.reshape(n_grp, SUBLANES, D_LRU)
        b3 = bterm.reshape(n_grp, SUBLANES, D_LRU)
        k = 1
        while k < SUBLANES:
            a_prev = jnp.where(sub >= k, pltpu.roll(a3, k, 1), 1.0)
            b_prev = jnp.where(sub >= k, pltpu.roll(b3, k, 1), 0.0)
            b3 = a3 * b_prev + b3
            a3 = a3 * a_prev
            k *= 2
        carry = lru["carry"]
        h_groups = []
        for grp_i in range(n_grp):
            h_grp = a3[grp_i] * carry + b3[grp_i]
            carry = h_grp[SUBLANES - 1:SUBLANES, :]
            h_groups.append(h_grp)
        lru["carry"] = carry
        h = jnp.concatenate(h_groups, axis=0)
        gb = u[rows, o1 + D_LRU:o1 + 2 * D_LRU]
        y[rows, D_CONV:D_CONV + D_LRU] = (h * jax.nn.gelu(gb)).astype(BF16)

    k_dup = kv[0:N_KV_HEADS]
    v_lo = kv[N_KV_HEADS:2 * N_KV_HEADS]
    v_hi = kv[2 * N_KV_HEADS:3 * N_KV_HEADS]
    heads_per_kv = N_Q_HEADS // N_KV_HEADS

    def kv_prep():
        kk = u[:, o3:o3 + KV_DIM]
        vv = u[:, o3 + KV_DIM:o3 + 2 * KV_DIM]
        lo = lax.broadcasted_iota(jnp.int32, (T, LANES), 1) < HEAD_DIM
        k_rot = pltpu.roll(kk, HEAD_DIM, 1)
        v_rot = pltpu.roll(vv, HEAD_DIM, 1)
        new_rows = slice(BLOCK, BLOCK + T)
        k_dup[0][new_rows, :] = jnp.where(lo, kk, k_rot).astype(BF16)
        k_dup[1][new_rows, :] = jnp.where(lo, k_rot, kk).astype(BF16)
        v_lo[0][new_rows, :] = jnp.where(lo, vv, 0.0).astype(BF16)
        v_hi[0][new_rows, :] = jnp.where(lo, 0.0, v_rot).astype(BF16)
        v_lo[1][new_rows, :] = jnp.where(lo, v_rot, 0.0).astype(BF16)
        v_hi[1][new_rows, :] = jnp.where(lo, 0.0, vv).astype(BF16)

    def attn(qb, grp):
        qi = lax.broadcasted_iota(jnp.int32, (BLOCK, 2 * BLOCK), 0)
        kj = lax.broadcasted_iota(jnp.int32, (BLOCK, 2 * BLOCK), 1)
        dist = qi - kj + BLOCK
        valid = pltpu.bitcast(dist, jnp.uint32) < WINDOW
        if qb == 0:
            valid = valid & (kj >= jnp.where(s_idx == 0, BLOCK, 0))
        qlane_lo = lax.broadcasted_iota(jnp.int32, (BLOCK, LANES), 1) < HEAD_DIM
        r0 = qb * BLOCK
        q_rows = slice(r0, r0 + BLOCK)
        band = slice(r0, r0 + 2 * BLOCK)
        chunks = range(grp * heads_per_kv // 2, (grp + 1) * heads_per_kv // 2)
        q_parts = []
        for c in chunks:
            qc = u[q_rows, o2 + c * LANES:o2 + (c + 1) * LANES] * (HEAD_DIM ** -0.5)
            q_parts += [jnp.where(qlane_lo, qc, 0.0).astype(BF16), jnp.where(qlane_lo, 0.0, qc).astype(BF16)]
        s_all = _dot_nt(jnp.concatenate(q_parts, axis=0), k_dup[grp][band, :])
        probs = []
        for hh in range(heads_per_kv):
            head = grp * heads_per_kv + hh
            s = s_all[hh * BLOCK:(hh + 1) * BLOCK, :] + bias_tab[head]
            s = jnp.where(valid, s, -1e30)
            sink = sink_ref[layer, head]
            m = jnp.maximum(jnp.max(s, axis=-1, keepdims=True), sink)
            e = jnp.exp(s - m)
            den = jnp.sum(e, axis=-1, keepdims=True) + jnp.exp(sink - m)
            probs.append((e / den).astype(BF16))
        p_rows = [jnp.concatenate(probs[2 * i:2 * i + 2], axis=1) for i in range(heads_per_kv // 2)]
        v_both = jnp.concatenate([v_lo[grp][band, :], v_hi[grp][band, :]], axis=0)
        o_pairs = _dot(jnp.concatenate(p_rows, axis=0), v_both)
        for i, c in enumerate(chunks):
            col = D_CONV + D_LRU + c * LANES
            y[q_rows, col:col + LANES] = o_pairs[i * BLOCK:(i + 1) * BLOCK, :].astype(BF16)

    def out_proj(qb):
        q_rows = slice(qb * BLOCK, (qb + 1) * BLOCK)
        hproj = _dot(y[q_rows, :], wout_ref[...])
        o_ref[0, q_rows, :] = ALPHA * x_ref[0, q_rows, :] + hproj

    in_proj()
    conv_prep()
    for rb in range(T // CONV_ROWS):
        conv_block(rb)
    lru_prep()
    for rb in range(T // LRU_ROWS):
        lru_block(rb)
    kv_prep()
    for qb in range(T // BLOCK):
        for grp in range(N_KV_HEADS):
            attn(qb, grp)
        out_proj(qb)
    hcar[...] = lru["carry"]
    for buf in kv:
        buf[0:BLOCK, :] = buf[T:T + BLOCK, :]


def _mixer_call(x, layer, w_in, w_out, bucket, rel_bias, cw, cb, cg, cbeta, lw, lb, wa, ba, wx, bx, lam, sinks):
    B, S, _ = x.shape
    T = SEQ_TILE
    x_spec = pl.BlockSpec((1, T, D_MODEL), lambda bi, si: (bi, si, 0))
    lay = (layer,)
    smem = pl.BlockSpec(memory_space=pltpu.SMEM)
    in_specs = [
        x_spec,
        _resident((D_MODEL, D_IN), lay),
        _resident((D_MODEL, D_MODEL), lay),
        _resident((BLOCK, 2 * BLOCK)),
        smem,
        _resident((CONV_KERNEL, D_CONV), lay), _resident((1, D_CONV), lay),
        _resident((1, D_CONV), lay), _resident((1, D_CONV), lay),
        _resident((LRU_CONV, D_LRU), lay), _resident((1, D_LRU), lay),
        _resident((D_LRU, D_LRU), lay), _resident((1, D_LRU), lay),
        _resident((D_LRU, D_LRU), lay), _resident((1, D_LRU), lay),
        _resident((1, D_LRU), lay),
        smem,
    ]
    scratch = (
        [pltpu.VMEM((N_Q_HEADS, BLOCK, 2 * BLOCK), F32),
         pltpu.VMEM((CONV_HIST, D_CONV), F32),
         pltpu.VMEM((LRU_HIST, D_LRU), F32),
         pltpu.VMEM((1, D_LRU), F32),
         pltpu.VMEM((T, D_IN), F32),
         pltpu.VMEM((T, D_MODEL), BF16),
         pltpu.VMEM((SUBLANES, CONV_HIST + T, D_CONV), F32),
         pltpu.VMEM((LRU_HIST + T, D_LRU), F32)]
        + [pltpu.VMEM((BLOCK + T, KV_DIM), BF16) for _ in range(N_KV_BUFS)])
    return pl.pallas_call(
        functools.partial(_mixer_kernel, layer),
        grid=(B, S // T),
        in_specs=in_specs,
        out_specs=x_spec,
        out_shape=jax.ShapeDtypeStruct(x.shape, F32),
        scratch_shapes=scratch,
        compiler_params=pltpu.CompilerParams(
            dimension_semantics=("arbitrary", "arbitrary"), vmem_limit_bytes=VMEM_LIMIT_BYTES),
        name="mixer_ln",
    )(x, w_in, w_out, bucket, rel_bias, cw, cb, cg, cbeta, lw, lb, wa, ba, wx, bx, lam, sinks)


def _bucket_table():
    qi = np.arange(BLOCK)[:, None]
    kj = np.arange(2 * BLOCK)[None, :]
    dist = np.maximum(qi - kj + BLOCK, 0)
    max_exact = REL_BUCKETS // 2
    ratio = np.log(np.maximum(dist, 1).astype(np.float32) / np.float32(max_exact)) / np.float32(
        math.log(REL_MAX_DIST / max_exact))
    large = max_exact + (ratio * np.float32(REL_BUCKETS - max_exact)).astype(np.int32)
    return np.where(dist < max_exact, dist, np.minimum(large, REL_BUCKETS - 1)).astype(np.int32)


def _block_diag(w):
    l, h, d, _ = w.shape
    eye = jnp.eye(h, dtype=w.dtype)
    return (eye[None, :, None, :, None] * w[:, :, :, None, :]).reshape(l, h * d, h * d)


def kernel(x, rel_bias, ln_g, ln_b, ffn_w_gate, ffn_w_up, ffn_w_down, w_in, conv_dw_w, conv_dw_b, conv_ln_g, conv_ln_b, lru_conv_w, lru_conv_b, lru_wa, lru_ba, lru_wx, lru_bx, lru_lambda, attn_sinks, w_out):
    B, S, D = x.shape
    assert D == D_MODEL and S % SEQ_TILE == 0 and (B * S) % FFN_TILE == 0
    rows = lambda v: v.reshape(v.shape[:-1] + (1, v.shape[-1]))
    wg, wu, wd = ffn_w_gate, ffn_w_up, ffn_w_down
    w_in_b, w_out_b = w_in.astype(BF16), w_out.astype(BF16)
    wa, wx = _block_diag(lru_wa).astype(BF16), _block_diag(lru_wx).astype(BF16)
    ln = jnp.stack([ln_g, ln_b], axis=2)
    bucket = _bucket_table()
    for l in range(DEPTH):
        x = _ffn_call(x.reshape(B * S, D), wg, wu, wd, ln, l, 0, norm_input=False).reshape(B, S, D)
        z = _mixer_call(
            x, l, w_in_b, w_out_b, bucket, rel_bias,
            conv_dw_w, rows(conv_dw_b), rows(conv_ln_g), rows(conv_ln_b),
            lru_conv_w, rows(lru_conv_b), wa, rows(lru_ba), wx, rows(lru_bx), rows(lru_lambda),
            attn_sinks)
        x = _ffn_call(z.reshape(B * S, D), wg, wu, wd, ln, l, 1, norm_input=True).reshape(B, S, D)
    return x
```

```python
import functools
import math

import jax
import jax.numpy as jnp
import numpy as np
from jax import lax
from jax.experimental import pallas as pl
from jax.experimental.pallas import tpu as pltpu

D_MODEL = 1024
DEPTH = 2
D_CONV = 256
D_LRU = 256
D_ATTN = 512
CONV_KERNEL = 31
LRU_CONV = 4
LRU_C = 8.0
HEAD_DIM = 64
N_Q_HEADS = 8
N_KV_HEADS = 2
KV_DIM = 128
WINDOW = 128
BLOCK = 128
REL_BUCKETS = 32
REL_MAX_DIST = 128
D_FF = 2816
ALPHA = (2.0 * DEPTH) ** 0.25
LN_EPS = 1e-5
D_IN = 2 * D_CONV + 2 * D_LRU + D_ATTN + 2 * KV_DIM

LANES = 128
SUBLANES = 8
FF_CHUNK = 256
FFN_TILE = 512
SEQ_TILE = 512
CONV_ROWS = 128
LRU_ROWS = 128
W_SLOTS = 3
W_COLS = 256
CONV_HIST = 32
LRU_HIST = 8
N_KV_BUFS = 6
VMEM_LIMIT_BYTES = 56 * 1024 * 1024

F32 = jnp.float32
BF16 = jnp.bfloat16


def _layernorm(z, g, b):
    mu = jnp.mean(z, axis=-1, keepdims=True)
    zc = z - mu
    var = jnp.mean(zc * zc, axis=-1, keepdims=True)
    return zc * lax.rsqrt(var + LN_EPS) * g + b


def _dot(a, b):
    return jnp.dot(a, b, preferred_element_type=F32)


def _dot_nt(a, b):
    return lax.dot_general(a, b, (((1,), (1,)), ((), ())), preferred_element_type=F32)


def _resident(shape, lead=()):
    block = (None,) * len(lead) + tuple(shape)
    index = tuple(lead) + (0,) * len(shape)
    return pl.BlockSpec(block, lambda *_: index, pipeline_mode=pl.Buffered(1))


def _zero_from(*values):
    word = None
    for v in values:
        bits = pltpu.bitcast(v, jnp.uint32)
        for grp in range(v.shape[0] // SUBLANES):
            for blk in range(v.shape[1] // LANES):
                piece = bits[grp * SUBLANES:(grp + 1) * SUBLANES, blk * LANES:(blk + 1) * LANES]
                word = piece if word is None else word | piece
    zero = lax.shift_right_logical(lax.shift_right_logical(word, jnp.uint32(16)), jnp.uint32(16))
    return pltpu.bitcast(zero, F32)[0:1, 0:1]


def _ffn_kernel(n_tiles, layer, half, norm_input, *refs):
    if norm_input:
        (x0_ref, xnext_ref, wg_hbm, wu_hbm, wd_hbm, ln_ref, o_ref,
         h_ref, z_ref, wg_ref, wu_ref, wd_ref, stage_g, stage_u, stage_d, sems, xn_ref, xn_next_ref) = refs
    else:
        (xn_ref, wg_hbm, wu_hbm, wd_hbm, ln_ref, o_ref,
         h_ref, z_ref, wg_ref, wu_ref, wd_ref, stage_g, stage_u, stage_d, sems) = refs
    i = pl.program_id(0)
    n_chunks = D_FF // FF_CHUNK
    k_out = 2 * half

    def normalise_previous():
        out = _layernorm(z_ref[...], ln_ref[k_out, 0:1, :], ln_ref[k_out, 1:2, :])
        o_ref[...] = out
        return out

    def normalise_input(src_ref):
        return _layernorm(src_ref[...], ln_ref[1, 0:1, :], ln_ref[1, 1:2, :])

    def swiglu_chunk(xb, c, extra=None):
        sl = slice(c * FF_CHUNK, (c + 1) * FF_CHUNK)
        gate = _dot(xb, wg_ref[:, sl])
        up = _dot(xb, wu_ref[:, sl])
        if extra is not None:
            up = up + extra
        h_ref[:, sl] = (gate * jax.nn.sigmoid(gate) * up).astype(BF16)

    def down_and_residual():
        y = _dot(h_ref[...], wd_ref[...])
        z_ref[...] = ALPHA * xn_ref[...] + 0.5 * y

    def weight_copies(c, slot):
        cols = pl.ds(c * FF_CHUNK, FF_CHUNK)
        return (
            pltpu.make_async_copy(wg_hbm.at[layer, half, :, cols], stage_g.at[slot], sems.at[slot, 0]),
            pltpu.make_async_copy(wu_hbm.at[layer, half, :, cols], stage_u.at[slot], sems.at[slot, 1]),
            pltpu.make_async_copy(wd_hbm.at[layer, half, cols, :], stage_d.at[slot], sems.at[slot, 2]),
        )

    @pl.when(i == 0)
    def _():
        for c in range(W_SLOTS - 1):
            for cp in weight_copies(c, c):
                cp.start()
        if norm_input:
            xn_ref[...] = normalise_input(x0_ref)
            xn_next_ref[...] = normalise_input(xnext_ref)
        xb = xn_ref[...].astype(BF16)
        for c in range(n_chunks):
            slot = c % W_SLOTS
            ahead = c + W_SLOTS - 1
            if ahead < n_chunks:
                for cp in weight_copies(ahead, ahead % W_SLOTS):
                    cp.start()
            for cp in weight_copies(c, slot):
                cp.wait()
            sl = slice(c * FF_CHUNK, (c + 1) * FF_CHUNK)
            wg_ref[:, sl] = stage_g[slot].astype(BF16)
            wu_ref[:, sl] = stage_u[slot].astype(BF16)
            wd_ref[sl, :] = stage_d[slot].astype(BF16)
            swiglu_chunk(xb, c)
        down_and_residual()

    @pl.when((i > 0) & (i < n_tiles))
    def _():
        done = [normalise_previous()]
        if norm_input:
            xn_ref[...] = xn_next_ref[...]
            nxt = normalise_input(xnext_ref)
            xn_next_ref[...] = nxt
            done.append(nxt)
        zero_rows = jnp.broadcast_to(_zero_from(*done), (FFN_TILE, FF_CHUNK))
        xb = xn_ref[...].astype(BF16)
        for c in range(n_chunks):
            swiglu_chunk(xb, c, zero_rows if c == n_chunks - 1 else None)
        down_and_residual()

    @pl.when(i == n_tiles)
    def _():
        normalise_previous()


def _ffn_call(x2d, wg, wu, wd, ln, layer, half, norm_input):
    n_tok = x2d.shape[0]
    n_tiles = n_tok // FFN_TILE
    hbm = pl.BlockSpec(memory_space=pl.ANY)
    tile = (FFN_TILE, D_MODEL)
    if norm_input:
        x_specs = [pl.BlockSpec(tile, lambda i: (0, 0)),
                   pl.BlockSpec(tile, lambda i: (jnp.minimum(i + 1, n_tiles - 1), 0))]
        x_args = [x2d, x2d]
        x_scratch = [pltpu.VMEM(tile, F32), pltpu.VMEM(tile, F32)]
    else:
        x_specs = [pl.BlockSpec(tile, lambda i: (jnp.minimum(i, n_tiles - 1), 0))]
        x_args = [x2d]
        x_scratch = []
    return pl.pallas_call(
        functools.partial(_ffn_kernel, n_tiles, layer, half, norm_input),
        grid=(n_tiles + 1,),
        in_specs=x_specs + [hbm, hbm, hbm, _resident((3, 2, D_MODEL), (layer,))],
        out_specs=pl.BlockSpec(tile, lambda i: (jnp.maximum(i - 1, 0), 0)),
        out_shape=jax.ShapeDtypeStruct((n_tok, D_MODEL), F32),
        scratch_shapes=[
            pltpu.VMEM((FFN_TILE, D_FF), BF16),
            pltpu.VMEM(tile, F32),
            pltpu.VMEM((D_MODEL, D_FF), BF16),
            pltpu.VMEM((D_MODEL, D_FF), BF16),
            pltpu.VMEM((D_FF, D_MODEL), BF16),
            pltpu.VMEM((W_SLOTS, D_MODEL, FF_CHUNK), F32),
            pltpu.VMEM((W_SLOTS, D_MODEL, FF_CHUNK), F32),
            pltpu.VMEM((W_SLOTS, FF_CHUNK, D_MODEL), F32),
            pltpu.SemaphoreType.DMA((W_SLOTS, 3)),
        ] + x_scratch,
        compiler_params=pltpu.CompilerParams(
            dimension_semantics=("arbitrary",), vmem_limit_bytes=VMEM_LIMIT_BYTES),
        name="ffn_ln",
    )(*x_args, wg, wu, wd, ln)


def _mixer_kernel(layer,
                  x_ref, win_hbm, wout_hbm, bucket_ref, relb_ref,
                  cw_ref, cb_ref, cg_ref, cbeta_ref,
                  lw_ref, lb_ref, wa_ref, ba_ref, wx_ref, bx_ref, lam_ref,
                  sink_ref,
                  o_ref,
                  bias_tab, chist, xhist, hcar, u, y, cs, xb, win_ref, wout_ref, w_stage, w_sems, *kv):
    T = SEQ_TILE
    s_idx = pl.program_id(1)
    n_in_chunks = D_IN // W_COLS
    n_w_chunks = n_in_chunks + D_MODEL // W_COLS

    def weight_copy(k, slot):
        src = (win_hbm.at[layer, :, pl.ds(k * W_COLS, W_COLS)] if k < n_in_chunks
               else wout_hbm.at[layer, :, pl.ds((k - n_in_chunks) * W_COLS, W_COLS)])
        return pltpu.make_async_copy(src, w_stage.at[slot], w_sems.at[slot])

    @pl.when((pl.program_id(0) == 0) & (s_idx == 0))
    def _first_step():
        weight_copy(0, 0).start()
        bucket = bucket_ref[...]
        for head in range(N_Q_HEADS):
            tab = jnp.zeros((BLOCK, 2 * BLOCK), F32)
            for bkt in range(REL_BUCKETS):
                tab = jnp.where(bucket == bkt, relb_ref[bkt, head], tab)
            bias_tab[head] = tab
        for k in range(n_w_chunks):
            slot = k % 2
            if k + 1 < n_w_chunks:
                weight_copy(k + 1, 1 - slot).start()
            weight_copy(k, slot).wait()
            if k < n_in_chunks:
                win_ref[:, k * W_COLS:(k + 1) * W_COLS] = w_stage[slot].astype(BF16)
            else:
                kk = k - n_in_chunks
                wout_ref[:, kk * W_COLS:(kk + 1) * W_COLS] = w_stage[slot].astype(BF16)

    @pl.when(s_idx == 0)
    def _start_of_sequence():
        chist[...] = jnp.zeros((CONV_HIST, D_CONV), F32)
        xhist[...] = jnp.zeros((LRU_HIST, D_LRU), F32)
        hcar[...] = jnp.zeros((1, D_LRU), F32)
        for buf in kv:
            buf[0:BLOCK, :] = jnp.zeros((BLOCK, KV_DIM), BF16)

    o1 = 2 * D_CONV
    o2 = o1 + 2 * D_LRU
    o3 = o2 + D_ATTN

    def in_proj():
        x_bf = x_ref[0].astype(BF16)
        for c0, c1 in ((0, o1), (o1, o2), (o3, D_IN), (o2, o3)):
            u[:, c0:c1] = _dot(x_bf, win_ref[:, c0:c1])

    def conv_prep():
        yglu = u[:, 0:D_CONV] * jax.nn.sigmoid(u[:, D_CONV:2 * D_CONV])
        n_rows = CONV_HIST + T
        whole = jnp.concatenate([chist[...], yglu], axis=0)
        cs[0] = whole
        for r in range(1, SUBLANES):
            cs[r] = pltpu.roll(whole, n_rows - r, 0)
        chist[...] = yglu[T - CONV_HIST:T, :]

    def conv_block(rb):
        off = CONV_HIST - (CONV_KERNEL - 1)
        acc = jnp.broadcast_to(cb_ref[...], (CONV_ROWS, D_CONV))
        for tap in range(CONV_KERNEL):
            r = (off + tap) % SUBLANES
            first = rb * CONV_ROWS + off + tap - r
            acc = acc + cw_ref[tap:tap + 1, :] * cs[r, first:first + CONV_ROWS, :]
        yc = _layernorm(acc, cg_ref[...], cbeta_ref[...])
        y[rb * CONV_ROWS:(rb + 1) * CONV_ROWS, 0:D_CONV] = (yc * jax.nn.sigmoid(yc)).astype(BF16)

    lru = {}

    def lru_prep():
        lam = lam_ref[...]
        lru["log_sig"] = -(jnp.maximum(-lam, 0.0) + jnp.log1p(jnp.exp(-jnp.abs(lam))))
        x_new = u[:, o1:o1 + D_LRU]
        xb[0:LRU_HIST, :] = xhist[...]
        xb[LRU_HIST:LRU_HIST + T, :] = x_new
        xhist[...] = x_new[T - LRU_HIST:T, :]
        lru["carry"] = hcar[...]

    def lru_block(rb):
        off = LRU_HIST - (LRU_CONV - 1)
        halo = LRU_ROWS + SUBLANES
        n_grp = LRU_ROWS // SUBLANES
        sub = lax.broadcasted_iota(jnp.int32, (n_grp, SUBLANES, D_LRU), 1)
        rows = slice(rb * LRU_ROWS, (rb + 1) * LRU_ROWS)
        blk = xb[rb * LRU_ROWS:rb * LRU_ROWS + halo, :]
        xc = jnp.broadcast_to(lb_ref[...], (LRU_ROWS, D_LRU))
        for tap in range(LRU_CONV):
            r = (off + tap) % SUBLANES
            first = off + tap - r
            src = blk if r == 0 else pltpu.roll(blk, halo - r, 0)
            xc = xc + lw_ref[tap:tap + 1, :] * src[first:first + LRU_ROWS, :]
        xcb = xc.astype(BF16)
        r_gate = jax.nn.sigmoid(_dot(xcb, wa_ref[...]) + ba_ref[...])
        i_gate = jax.nn.sigmoid(_dot(xcb, wx_ref[...]) + bx_ref[...])
        log_a = LRU_C * r_gate * lru["log_sig"]
        a = jnp.exp(log_a)
        mult = jnp.sqrt(-jnp.tanh(log_a) * (a * a + 1.0))
        bterm = mult * (i_gate * xc)
        a3 = a.reshape(n_grp, SUBLANES, D_LRU)
        b3 = bterm.reshape(n_grp, SUBLANES, D_LRU)
        k = 1
        while k < SUBLANES:
            a_prev = jnp.where(sub >= k, pltpu.roll(a3, k, 1), 1.0)
            b_prev = jnp.where(sub >= k, pltpu.roll(b3, k, 1), 0.0)
            b3 = a3 * b_prev + b3
            a3 = a3 * a_prev
            k *= 2
        carry = lru["carry"]
        h_groups = []
        for grp_i in range(n_grp):
            h_grp = a3[grp_i] * carry + b3[grp_i]
            carry = h_grp[SUBLANES - 1:SUBLANES, :]
            h_groups.append(h_grp)
        lru["carry"] = carry
        h = jnp.concatenate(h_groups, axis=0)
        gb = u[rows, o1 + D_LRU:o1 + 2 * D_LRU]
        y[rows, D_CONV:D_CONV + D_LRU] = (h * jax.nn.gelu(gb)).astype(BF16)

    k_dup = kv[0:N_KV_HEADS]
    v_lo = kv[N_KV_HEADS:2 * N_KV_HEADS]
    v_hi = kv[2 * N_KV_HEADS:3 * N_KV_HEADS]
    heads_per_kv = N_Q_HEADS // N_KV_HEADS

    def kv_prep():
        kk = u[:, o3:o3 + KV_DIM]
        vv = u[:, o3 + KV_DIM:o3 + 2 * KV_DIM]
        lo = lax.broadcasted_iota(jnp.int32, (T, LANES), 1) < HEAD_DIM
        k_rot = pltpu.roll(kk, HEAD_DIM, 1)
        v_rot = pltpu.roll(vv, HEAD_DIM, 1)
        new_rows = slice(BLOCK, BLOCK + T)
        k_dup[0][new_rows, :] = jnp.where(lo, kk, k_rot).astype(BF16)
        k_dup[1][new_rows, :] = jnp.where(lo, k_rot, kk).astype(BF16)
        v_lo[0][new_rows, :] = jnp.where(lo, vv, 0.0).astype(BF16)
        v_hi[0][new_rows, :] = jnp.where(lo, 0.0, v_rot).astype(BF16)
        v_lo[1][new_rows, :] = jnp.where(lo, v_rot, 0.0).astype(BF16)
        v_hi[1][new_rows, :] = jnp.where(lo, 0.0, vv).astype(BF16)

    def attn(qb, grp):
        qi = lax.broadcasted_iota(jnp.int32, (BLOCK, 2 * BLOCK), 0)
        kj = lax.broadcasted_iota(jnp.int32, (BLOCK, 2 * BLOCK), 1)
        dist = qi - kj + BLOCK
        valid = pltpu.bitcast(dist, jnp.uint32) < WINDOW
        if qb == 0:
            valid = valid & (kj >= jnp.where(s_idx == 0, BLOCK, 0))
        qlane_lo = lax.broadcasted_iota(jnp.int32, (BLOCK, LANES), 1) < HEAD_DIM
        r0 = qb * BLOCK
        q_rows = slice(r0, r0 + BLOCK)
        band = slice(r0, r0 + 2 * BLOCK)
        chunks = range(grp * heads_per_kv // 2, (grp + 1) * heads_per_kv // 2)
        q_parts = []
        for c in chunks:
            qc = u[q_rows, o2 + c * LANES:o2 + (c + 1) * LANES] * (HEAD_DIM ** -0.5)
            q_parts += [jnp.where(qlane_lo, qc, 0.0).astype(BF16), jnp.where(qlane_lo, 0.0, qc).astype(BF16)]
        s_all = _dot_nt(jnp.concatenate(q_parts, axis=0), k_dup[grp][band, :])
        probs = []
        for hh in range(heads_per_kv):
            head = grp * heads_per_kv + hh
            s = s_all[hh * BLOCK:(hh + 1) * BLOCK, :] + bias_tab[head]
            s = jnp.where(valid, s, -1e30)
            sink = sink_ref[layer, head]
            m = jnp.maximum(jnp.max(s, axis=-1, keepdims=True), sink)
            e = jnp.exp(s - m)
            den = jnp.sum(e, axis=-1, keepdims=True) + jnp.exp(sink - m)
            probs.append((e / den).astype(BF16))
        p_rows = [jnp.concatenate(probs[2 * i:2 * i + 2], axis=1) for i in range(heads_per_kv // 2)]
        v_both = jnp.concatenate([v_lo[grp][band, :], v_hi[grp][band, :]], axis=0)
        o_pairs = _dot(jnp.concatenate(p_rows, axis=0), v_both)
        for i, c in enumerate(chunks):
            col = D_CONV + D_LRU + c * LANES
            y[q_rows, col:col + LANES] = o_pairs[i * BLOCK:(i + 1) * BLOCK, :].astype(BF16)

    def out_proj(qb):
        q_rows = slice(qb * BLOCK, (qb + 1) * BLOCK)
        hproj = _dot(y[q_rows, :], wout_ref[...])
        o_ref[0, q_rows, :] = ALPHA * x_ref[0, q_rows, :] + hproj

    in_proj()
    conv_prep()
    for rb in range(T // CONV_ROWS):
        conv_block(rb)
    lru_prep()
    for rb in range(T // LRU_ROWS):
        lru_block(rb)
    kv_prep()
    for qb in range(T // BLOCK):
        for grp in range(N_KV_HEADS):
            attn(qb, grp)
        out_proj(qb)
    hcar[...] = lru["carry"]
    for buf in kv:
        buf[0:BLOCK, :] = buf[T:T + BLOCK, :]


def _mixer_call(x, layer, w_in, w_out, bucket, rel_bias, cw, cb, cg, cbeta, lw, lb, wa, ba, wx, bx, lam, sinks):
    B, S, _ = x.shape
    T = SEQ_TILE
    x_spec = pl.BlockSpec((1, T, D_MODEL), lambda bi, si: (bi, si, 0))
    lay = (layer,)
    smem = pl.BlockSpec(memory_space=pltpu.SMEM)
    hbm = pl.BlockSpec(memory_space=pl.ANY)
    in_specs = [
        x_spec,
        hbm,
        hbm,
        _resident((BLOCK, 2 * BLOCK)),
        smem,
        _resident((CONV_KERNEL, D_CONV), lay), _resident((1, D_CONV), lay),
        _resident((1, D_CONV), lay), _resident((1, D_CONV), lay),
        _resident((LRU_CONV, D_LRU), lay), _resident((1, D_LRU), lay),
        _resident((D_LRU, D_LRU), lay), _resident((1, D_LRU), lay),
        _resident((D_LRU, D_LRU), lay), _resident((1, D_LRU), lay),
        _resident((1, D_LRU), lay),
        smem,
    ]
    scratch = (
        [pltpu.VMEM((N_Q_HEADS, BLOCK, 2 * BLOCK), F32),
         pltpu.VMEM((CONV_HIST, D_CONV), F32),
         pltpu.VMEM((LRU_HIST, D_LRU), F32),
         pltpu.VMEM((1, D_LRU), F32),
         pltpu.VMEM((T, D_IN), F32),
         pltpu.VMEM((T, D_MODEL), BF16),
         pltpu.VMEM((SUBLANES, CONV_HIST + T, D_CONV), F32),
         pltpu.VMEM((LRU_HIST + T, D_LRU), F32),
         pltpu.VMEM((D_MODEL, D_IN), BF16),
         pltpu.VMEM((D_MODEL, D_MODEL), BF16),
         pltpu.VMEM((2, D_MODEL, W_COLS), F32),
         pltpu.SemaphoreType.DMA((2,))]
        + [pltpu.VMEM((BLOCK + T, KV_DIM), BF16) for _ in range(N_KV_BUFS)])
    return pl.pallas_call(
        functools.partial(_mixer_kernel, layer),
        grid=(B, S // T),
        in_specs=in_specs,
        out_specs=x_spec,
        out_shape=jax.ShapeDtypeStruct(x.shape, F32),
        scratch_shapes=scratch,
        compiler_params=pltpu.CompilerParams(
            dimension_semantics=("arbitrary", "arbitrary"), vmem_limit_bytes=VMEM_LIMIT_BYTES),
        name="mixer_ln",
    )(x, w_in, w_out, bucket, rel_bias, cw, cb, cg, cbeta, lw, lb, wa, ba, wx, bx, lam, sinks)


def _bucket_table():
    qi = np.arange(BLOCK)[:, None]
    kj = np.arange(2 * BLOCK)[None, :]
    dist = np.maximum(qi - kj + BLOCK, 0)
    max_exact = REL_BUCKETS // 2
    ratio = np.log(np.maximum(dist, 1).astype(np.float32) / np.float32(max_exact)) / np.float32(
        math.log(REL_MAX_DIST / max_exact))
    large = max_exact + (ratio * np.float32(REL_BUCKETS - max_exact)).astype(np.int32)
    return np.where(dist < max_exact, dist, np.minimum(large, REL_BUCKETS - 1)).astype(np.int32)


def _block_diag(w):
    l, h, d, _ = w.shape
    eye = jnp.eye(h, dtype=w.dtype)
    return (eye[None, :, None, :, None] * w[:, :, :, None, :]).reshape(l, h * d, h * d)


def kernel(x, rel_bias, ln_g, ln_b, ffn_w_gate, ffn_w_up, ffn_w_down, w_in, conv_dw_w, conv_dw_b, conv_ln_g, conv_ln_b, lru_conv_w, lru_conv_b, lru_wa, lru_ba, lru_wx, lru_bx, lru_lambda, attn_sinks, w_out):
    B, S, D = x.shape
    assert D == D_MODEL and S % SEQ_TILE == 0 and (B * S) % FFN_TILE == 0
    rows = lambda v: v.reshape(v.shape[:-1] + (1, v.shape[-1]))
    wg, wu, wd = ffn_w_gate, ffn_w_up, ffn_w_down
    wa, wx = _block_diag(lru_wa).astype(BF16), _block_diag(lru_wx).astype(BF16)
    ln = jnp.stack([ln_g, ln_b], axis=2)
    bucket = _bucket_table()
    for l in range(DEPTH):
        x = _ffn_call(x.reshape(B * S, D), wg, wu, wd, ln, l, 0, norm_input=False).reshape(B, S, D)
        z = _mixer_call(
            x, l, w_in, w_out, bucket, rel_bias,
            conv_dw_w, rows(conv_dw_b), rows(conv_ln_g), rows(conv_ln_b),
            lru_conv_w, rows(lru_conv_b), wa, rows(lru_ba), wx, rows(lru_bx), rows(lru_lambda),
            attn_sinks)
        x = _ffn_call(z.reshape(B * S, D), wg, wu, wd, ln, l, 1, norm_input=True).reshape(B, S, D)
    return x
```

```python
import functools
import math

import jax
import jax.numpy as jnp
import numpy as np
from jax import lax
from jax.experimental import pallas as pl
from jax.experimental.pallas import tpu as pltpu

D_MODEL = 1024
DEPTH = 2
D_CONV = 256
D_LRU = 256
D_ATTN = 512
CONV_KERNEL = 31
LRU_CONV = 4
LRU_C = 8.0
HEAD_DIM = 64
N_Q_HEADS = 8
N_KV_HEADS = 2
KV_DIM = 128
WINDOW = 128
BLOCK = 128
REL_BUCKETS = 32
REL_MAX_DIST = 128
D_FF = 2816
ALPHA = (2.0 * DEPTH) ** 0.25
LN_EPS = 1e-5
D_IN = 2 * D_CONV + 2 * D_LRU + D_ATTN + 2 * KV_DIM

LANES = 128
SUBLANES = 8
FF_CHUNK = 256
FFN_TILE = 512
SEQ_TILE = 512
CONV_ROWS = 512
LRU_ROWS = 128
W_SLOTS = 3
W_COLS = 256
CONV_HIST = 32
LRU_HIST = 8
N_KV_BUFS = 6
VMEM_LIMIT_BYTES = 56 * 1024 * 1024

F32 = jnp.float32
BF16 = jnp.bfloat16


def _layernorm(z, g, b):
    mu = jnp.mean(z, axis=-1, keepdims=True)
    zc = z - mu
    var = jnp.mean(zc * zc, axis=-1, keepdims=True)
    return zc * lax.rsqrt(var + LN_EPS) * g + b


def _dot(a, b):
    return jnp.dot(a, b, preferred_element_type=F32)


def _dot_nt(a, b):
    return lax.dot_general(a, b, (((1,), (1,)), ((), ())), preferred_element_type=F32)


def _resident(shape, lead=()):
    block = (None,) * len(lead) + tuple(shape)
    index = tuple(lead) + (0,) * len(shape)
    return pl.BlockSpec(block, lambda *_: index, pipeline_mode=pl.Buffered(1))


def _zero_from(*values):
    word = None
    for v in values:
        bits = pltpu.bitcast(v, jnp.uint32)
        for grp in range(v.shape[0] // SUBLANES):
            for blk in range(v.shape[1] // LANES):
                piece = bits[grp * SUBLANES:(grp + 1) * SUBLANES, blk * LANES:(blk + 1) * LANES]
                word = piece if word is None else word | piece
    zero = lax.shift_right_logical(lax.shift_right_logical(word, jnp.uint32(16)), jnp.uint32(16))
    return pltpu.bitcast(zero, F32)[0:1, 0:1]


def _ffn_kernel(n_tiles, layer, half, norm_input, *refs):
    if norm_input:
        (x0_ref, xnext_ref, wg_hbm, wu_hbm, wd_hbm, ln_ref, o_ref,
         h_ref, z_ref, wg_ref, wu_ref, wd_ref, stage_g, stage_u, stage_d, sems, xn_ref, xn_next_ref) = refs
    else:
        (xn_ref, wg_hbm, wu_hbm, wd_hbm, ln_ref, o_ref,
         h_ref, z_ref, wg_ref, wu_ref, wd_ref, stage_g, stage_u, stage_d, sems) = refs
    i = pl.program_id(0)
    n_chunks = D_FF // FF_CHUNK
    k_out = 2 * half

    def normalise_previous():
        out = _layernorm(z_ref[...], ln_ref[k_out, 0:1, :], ln_ref[k_out, 1:2, :])
        o_ref[...] = out
        return out

    def normalise_input(src_ref):
        return _layernorm(src_ref[...], ln_ref[1, 0:1, :], ln_ref[1, 1:2, :])

    def swiglu_chunk(xb, c, extra=None):
        sl = slice(c * FF_CHUNK, (c + 1) * FF_CHUNK)
        gate = _dot(xb, wg_ref[:, sl])
        up = _dot(xb, wu_ref[:, sl])
        if extra is not None:
            up = up + extra
        h_ref[:, sl] = (gate * jax.nn.sigmoid(gate) * up).astype(BF16)

    def down_and_residual():
        y = _dot(h_ref[...], wd_ref[...])
        z_ref[...] = ALPHA * xn_ref[...] + 0.5 * y

    def weight_copies(c, slot):
        cols = pl.ds(c * FF_CHUNK, FF_CHUNK)
        return (
            pltpu.make_async_copy(wg_hbm.at[layer, half, :, cols], stage_g.at[slot], sems.at[slot, 0]),
            pltpu.make_async_copy(wu_hbm.at[layer, half, :, cols], stage_u.at[slot], sems.at[slot, 1]),
            pltpu.make_async_copy(wd_hbm.at[layer, half, cols, :], stage_d.at[slot], sems.at[slot, 2]),
        )

    @pl.when(i == 0)
    def _():
        for c in range(W_SLOTS - 1):
            for cp in weight_copies(c, c):
                cp.start()
        if norm_input:
            xn_ref[...] = normalise_input(x0_ref)
            xn_next_ref[...] = normalise_input(xnext_ref)
        xb = xn_ref[...].astype(BF16)
        for c in range(n_chunks):
            slot = c % W_SLOTS
            ahead = c + W_SLOTS - 1
            if ahead < n_chunks:
                for cp in weight_copies(ahead, ahead % W_SLOTS):
                    cp.start()
            for cp in weight_copies(c, slot):
                cp.wait()
            sl = slice(c * FF_CHUNK, (c + 1) * FF_CHUNK)
            wg_ref[:, sl] = stage_g[slot].astype(BF16)
            wu_ref[:, sl] = stage_u[slot].astype(BF16)
            wd_ref[sl, :] = stage_d[slot].astype(BF16)
            swiglu_chunk(xb, c)
        down_and_residual()

    @pl.when((i > 0) & (i < n_tiles))
    def _():
        done = [normalise_previous()]
        if norm_input:
            xn_ref[...] = xn_next_ref[...]
            nxt = normalise_input(xnext_ref)
            xn_next_ref[...] = nxt
            done.append(nxt)
        zero_rows = jnp.broadcast_to(_zero_from(*done), (FFN_TILE, FF_CHUNK))
        xb = xn_ref[...].astype(BF16)
        for c in range(n_chunks):
            swiglu_chunk(xb, c, zero_rows if c == n_chunks - 1 else None)
        down_and_residual()

    @pl.when(i == n_tiles)
    def _():
        normalise_previous()


def _ffn_call(x2d, wg, wu, wd, ln, layer, half, norm_input):
    n_tok = x2d.shape[0]
    n_tiles = n_tok // FFN_TILE
    hbm = pl.BlockSpec(memory_space=pl.ANY)
    tile = (FFN_TILE, D_MODEL)
    if norm_input:
        x_specs = [pl.BlockSpec(tile, lambda i: (0, 0)),
                   pl.BlockSpec(tile, lambda i: (jnp.minimum(i + 1, n_tiles - 1), 0))]
        x_args = [x2d, x2d]
        x_scratch = [pltpu.VMEM(tile, F32), pltpu.VMEM(tile, F32)]
    else:
        x_specs = [pl.BlockSpec(tile, lambda i: (jnp.minimum(i, n_tiles - 1), 0))]
        x_args = [x2d]
        x_scratch = []
    return pl.pallas_call(
        functools.partial(_ffn_kernel, n_tiles, layer, half, norm_input),
        grid=(n_tiles + 1,),
        in_specs=x_specs + [hbm, hbm, hbm, _resident((3, 2, D_MODEL), (layer,))],
        out_specs=pl.BlockSpec(tile, lambda i: (jnp.maximum(i - 1, 0), 0)),
        out_shape=jax.ShapeDtypeStruct((n_tok, D_MODEL), F32),
        scratch_shapes=[
            pltpu.VMEM((FFN_TILE, D_FF), BF16),
            pltpu.VMEM(tile, F32),
            pltpu.VMEM((D_MODEL, D_FF), BF16),
            pltpu.VMEM((D_MODEL, D_FF), BF16),
            pltpu.VMEM((D_FF, D_MODEL), BF16),
            pltpu.VMEM((W_SLOTS, D_MODEL, FF_CHUNK), F32),
            pltpu.VMEM((W_SLOTS, D_MODEL, FF_CHUNK), F32),
            pltpu.VMEM((W_SLOTS, FF_CHUNK, D_MODEL), F32),
            pltpu.SemaphoreType.DMA((W_SLOTS, 3)),
        ] + x_scratch,
        compiler_params=pltpu.CompilerParams(
            dimension_semantics=("arbitrary",), vmem_limit_bytes=VMEM_LIMIT_BYTES),
        name="ffn_ln",
    )(*x_args, wg, wu, wd, ln)


def _mixer_kernel(layer,
                  x_ref, win_hbm, wout_hbm, bucket_ref, relb_ref,
                  cw_ref, cb_ref, cg_ref, cbeta_ref,
                  lw_ref, lb_ref, wa_ref, ba_ref, wx_ref, bx_ref, lam_ref,
                  sink_ref,
                  o_ref,
                  bias_tab, chist, xhist, hcar, u, y, cs, xb, win_ref, wout_ref, w_stage, w_sems, *kv):
    T = SEQ_TILE
    s_idx = pl.program_id(1)
    n_in_chunks = D_IN // W_COLS
    n_w_chunks = n_in_chunks + D_MODEL // W_COLS

    def weight_copy(k, slot):
        src = (win_hbm.at[layer, :, pl.ds(k * W_COLS, W_COLS)] if k < n_in_chunks
               else wout_hbm.at[layer, :, pl.ds((k - n_in_chunks) * W_COLS, W_COLS)])
        return pltpu.make_async_copy(src, w_stage.at[slot], w_sems.at[slot])

    @pl.when((pl.program_id(0) == 0) & (s_idx == 0))
    def _first_step():
        weight_copy(0, 0).start()
        bucket = bucket_ref[...]
        for head in range(N_Q_HEADS):
            tab = jnp.zeros((BLOCK, 2 * BLOCK), F32)
            for bkt in range(REL_BUCKETS):
                tab = jnp.where(bucket == bkt, relb_ref[bkt, head], tab)
            bias_tab[head] = tab
        for k in range(n_w_chunks):
            slot = k % 2
            if k + 1 < n_w_chunks:
                weight_copy(k + 1, 1 - slot).start()
            weight_copy(k, slot).wait()
            if k < n_in_chunks:
                win_ref[:, k * W_COLS:(k + 1) * W_COLS] = w_stage[slot].astype(BF16)
            else:
                kk = k - n_in_chunks
                wout_ref[:, kk * W_COLS:(kk + 1) * W_COLS] = w_stage[slot].astype(BF16)

    @pl.when(s_idx == 0)
    def _start_of_sequence():
        chist[...] = jnp.zeros((CONV_HIST, D_CONV), F32)
        xhist[...] = jnp.zeros((LRU_HIST, D_LRU), F32)
        hcar[...] = jnp.zeros((1, D_LRU), F32)
        for buf in kv:
            buf[0:BLOCK, :] = jnp.zeros((BLOCK, KV_DIM), BF16)

    o1 = 2 * D_CONV
    o2 = o1 + 2 * D_LRU
    o3 = o2 + D_ATTN

    def in_proj():
        x_bf = x_ref[0].astype(BF16)
        for c0, c1 in ((0, o1), (o1, o2), (o3, D_IN), (o2, o3)):
            u[:, c0:c1] = _dot(x_bf, win_ref[:, c0:c1])

    def conv_prep():
        yglu = u[:, 0:D_CONV] * jax.nn.sigmoid(u[:, D_CONV:2 * D_CONV])
        n_rows = CONV_HIST + T
        whole = jnp.concatenate([chist[...], yglu], axis=0)
        cs[0] = whole
        for r in range(1, SUBLANES):
            cs[r] = pltpu.roll(whole, n_rows - r, 0)
        chist[...] = yglu[T - CONV_HIST:T, :]

    def conv_block(rb):
        off = CONV_HIST - (CONV_KERNEL - 1)
        acc = jnp.broadcast_to(cb_ref[...], (CONV_ROWS, D_CONV))
        for tap in range(CONV_KERNEL):
            r = (off + tap) % SUBLANES
            first = rb * CONV_ROWS + off + tap - r
            acc = acc + cw_ref[tap:tap + 1, :] * cs[r, first:first + CONV_ROWS, :]
        yc = _layernorm(acc, cg_ref[...], cbeta_ref[...])
        y[rb * CONV_ROWS:(rb + 1) * CONV_ROWS, 0:D_CONV] = (yc * jax.nn.sigmoid(yc)).astype(BF16)

    lru = {}

    def lru_prep():
        lam = lam_ref[...]
        lru["log_sig"] = -(jnp.maximum(-lam, 0.0) + jnp.log1p(jnp.exp(-jnp.abs(lam))))
        x_new = u[:, o1:o1 + D_LRU]
        xb[0:LRU_HIST, :] = xhist[...]
        xb[LRU_HIST:LRU_HIST + T, :] = x_new
        xhist[...] = x_new[T - LRU_HIST:T, :]
        lru["carry"] = hcar[...]

    def lru_block(rb):
        off = LRU_HIST - (LRU_CONV - 1)
        halo = LRU_ROWS + SUBLANES
        n_grp = LRU_ROWS // SUBLANES
        sub = lax.broadcasted_iota(jnp.int32, (n_grp, SUBLANES, D_LRU), 1)
        rows = slice(rb * LRU_ROWS, (rb + 1) * LRU_ROWS)
        blk = xb[rb * LRU_ROWS:rb * LRU_ROWS + halo, :]
        xc = jnp.broadcast_to(lb_ref[...], (LRU_ROWS, D_LRU))
        for tap in range(LRU_CONV):
            r = (off + tap) % SUBLANES
            first = off + tap - r
            src = blk if r == 0 else pltpu.roll(blk, halo - r, 0)
            xc = xc + lw_ref[tap:tap + 1, :] * src[first:first + LRU_ROWS, :]
        xcb = xc.astype(BF16)
        r_gate = jax.nn.sigmoid(_dot(xcb, wa_ref[...]) + ba_ref[...])
        i_gate = jax.nn.sigmoid(_dot(xcb, wx_ref[...]) + bx_ref[...])
        log_a = LRU_C * r_gate * lru["log_sig"]
        a = jnp.exp(log_a)
        mult = jnp.sqrt(-jnp.tanh(log_a) * (a * a + 1.0))
        bterm = mult * (i_gate * xc)
        a3 = a.reshape(n_grp, SUBLANES, D_LRU)
        b3 = bterm.reshape(n_grp, SUBLANES, D_LRU)
        k = 1
        while k < SUBLANES:
            a_prev = jnp.where(sub >= k, pltpu.roll(a3, k, 1), 1.0)
            b_prev = jnp.where(sub >= k, pltpu.roll(b3, k, 1), 0.0)
            b3 = a3 * b_prev + b3
            a3 = a3 * a_prev
            k *= 2
        carry = lru["carry"]
        h_groups = []
        for grp_i in range(n_grp):
            h_grp = a3[grp_i] * carry + b3[grp_i]
            carry = h_grp[SUBLANES - 1:SUBLANES, :]
            h_groups.append(h_grp)
        lru["carry"] = carry
        h = jnp.concatenate(h_groups, axis=0)
        gb = u[rows, o1 + D_LRU:o1 + 2 * D_LRU]
        y[rows, D_CONV:D_CONV + D_LRU] = (h * jax.nn.gelu(gb)).astype(BF16)

    k_dup = kv[0:N_KV_HEADS]
    v_lo = kv[N_KV_HEADS:2 * N_KV_HEADS]
    v_hi = kv[2 * N_KV_HEADS:3 * N_KV_HEADS]
    heads_per_kv = N_Q_HEADS // N_KV_HEADS

    def kv_prep():
        kk = u[:, o3:o3 + KV_DIM]
        vv = u[:, o3 + KV_DIM:o3 + 2 * KV_DIM]
        lo = lax.broadcasted_iota(jnp.int32, (T, LANES), 1) < HEAD_DIM
        k_rot = pltpu.roll(kk, HEAD_DIM, 1)
        v_rot = pltpu.roll(vv, HEAD_DIM, 1)
        new_rows = slice(BLOCK, BLOCK + T)
        k_dup[0][new_rows, :] = jnp.where(lo, kk, k_rot).astype(BF16)
        k_dup[1][new_rows, :] = jnp.where(lo, k_rot, kk).astype(BF16)
        v_lo[0][new_rows, :] = jnp.where(lo, vv, 0.0).astype(BF16)
        v_hi[0][new_rows, :] = jnp.where(lo, 0.0, v_rot).astype(BF16)
        v_lo[1][new_rows, :] = jnp.where(lo, v_rot, 0.0).astype(BF16)
        v_hi[1][new_rows, :] = jnp.where(lo, 0.0, vv).astype(BF16)

    def attn(qb, grp):
        qi = lax.broadcasted_iota(jnp.int32, (BLOCK, 2 * BLOCK), 0)
        kj = lax.broadcasted_iota(jnp.int32, (BLOCK, 2 * BLOCK), 1)
        dist = qi - kj + BLOCK
        valid = pltpu.bitcast(dist, jnp.uint32) < WINDOW
        if qb == 0:
            valid = valid & (kj >= jnp.where(s_idx == 0, BLOCK, 0))
        qlane_lo = lax.broadcasted_iota(jnp.int32, (BLOCK, LANES), 1) < HEAD_DIM
        r0 = qb * BLOCK
        q_rows = slice(r0, r0 + BLOCK)
        band = slice(r0, r0 + 2 * BLOCK)
        chunks = range(grp * heads_per_kv // 2, (grp + 1) * heads_per_kv // 2)
        q_parts = []
        for c in chunks:
            qc = u[q_rows, o2 + c * LANES:o2 + (c + 1) * LANES] * (HEAD_DIM ** -0.5)
            q_parts += [jnp.where(qlane_lo, qc, 0.0).astype(BF16), jnp.where(qlane_lo, 0.0, qc).astype(BF16)]
        s_all = _dot_nt(jnp.concatenate(q_parts, axis=0), k_dup[grp][band, :])
        probs = []
        for hh in range(heads_per_kv):
            head = grp * heads_per_kv + hh
            s = s_all[hh * BLOCK:(hh + 1) * BLOCK, :] + bias_tab[head]
            s = jnp.where(valid, s, -1e30)
            sink = sink_ref[layer, head]
            m = jnp.maximum(jnp.max(s, axis=-1, keepdims=True), sink)
            e = jnp.exp(s - m)
            den = jnp.sum(e, axis=-1, keepdims=True) + jnp.exp(sink - m)
            probs.append((e / den).astype(BF16))
        p_rows = [jnp.concatenate(probs[2 * i:2 * i + 2], axis=1) for i in range(heads_per_kv // 2)]
        v_both = jnp.concatenate([v_lo[grp][band, :], v_hi[grp][band, :]], axis=0)
        o_pairs = _dot(jnp.concatenate(p_rows, axis=0), v_both)
        for i, c in enumerate(chunks):
            col = D_CONV + D_LRU + c * LANES
            y[q_rows, col:col + LANES] = o_pairs[i * BLOCK:(i + 1) * BLOCK, :].astype(BF16)

    def out_proj(qb):
        q_rows = slice(qb * BLOCK, (qb + 1) * BLOCK)
        hproj = _dot(y[q_rows, :], wout_ref[...])
        o_ref[0, q_rows, :] = ALPHA * x_ref[0, q_rows, :] + hproj

    in_proj()
    conv_prep()
    for rb in range(T // CONV_ROWS):
        conv_block(rb)
    lru_prep()
    for rb in range(T // LRU_ROWS):
        lru_block(rb)
    kv_prep()
    for qb in range(T // BLOCK):
        for grp in range(N_KV_HEADS):
            attn(qb, grp)
        out_proj(qb)
    hcar[...] = lru["carry"]
    for buf in kv:
        buf[0:BLOCK, :] = buf[T:T + BLOCK, :]


def _mixer_call(x, layer, w_in, w_out, bucket, rel_bias, cw, cb, cg, cbeta, lw, lb, wa, ba, wx, bx, lam, sinks):
    B, S, _ = x.shape
    T = SEQ_TILE
    x_spec = pl.BlockSpec((1, T, D_MODEL), lambda bi, si: (bi, si, 0))
    lay = (layer,)
    smem = pl.BlockSpec(memory_space=pltpu.SMEM)
    hbm = pl.BlockSpec(memory_space=pl.ANY)
    in_specs = [
        x_spec,
        hbm,
        hbm,
        _resident((BLOCK, 2 * BLOCK)),
        smem,
        _resident((CONV_KERNEL, D_CONV), lay), _resident((1, D_CONV), lay),
        _resident((1, D_CONV), lay), _resident((1, D_CONV), lay),
        _resident((LRU_CONV, D_LRU), lay), _resident((1, D_LRU), lay),
        _resident((D_LRU, D_LRU), lay), _resident((1, D_LRU), lay),
        _resident((D_LRU, D_LRU), lay), _resident((1, D_LRU), lay),
        _resident((1, D_LRU), lay),
        smem,
    ]
    scratch = (
        [pltpu.VMEM((N_Q_HEADS, BLOCK, 2 * BLOCK), F32),
         pltpu.VMEM((CONV_HIST, D_CONV), F32),
         pltpu.VMEM((LRU_HIST, D_LRU), F32),
         pltpu.VMEM((1, D_LRU), F32),
         pltpu.VMEM((T, D_IN), F32),
         pltpu.VMEM((T, D_MODEL), BF16),
         pltpu.VMEM((SUBLANES, CONV_HIST + T, D_CONV), F32),
         pltpu.VMEM((LRU_HIST + T, D_LRU), F32),
         pltpu.VMEM((D_MODEL, D_IN), BF16),
         pltpu.VMEM((D_MODEL, D_MODEL), BF16),
         pltpu.VMEM((2, D_MODEL, W_COLS), F32),
         pltpu.SemaphoreType.DMA((2,))]
        + [pltpu.VMEM((BLOCK + T, KV_DIM), BF16) for _ in range(N_KV_BUFS)])
    return pl.pallas_call(
        functools.partial(_mixer_kernel, layer),
        grid=(B, S // T),
        in_specs=in_specs,
        out_specs=x_spec,
        out_shape=jax.ShapeDtypeStruct(x.shape, F32),
        scratch_shapes=scratch,
        compiler_params=pltpu.CompilerParams(
            dimension_semantics=("arbitrary", "arbitrary"), vmem_limit_bytes=VMEM_LIMIT_BYTES),
        name="mixer_ln",
    )(x, w_in, w_out, bucket, rel_bias, cw, cb, cg, cbeta, lw, lb, wa, ba, wx, bx, lam, sinks)


def _bucket_table():
    qi = np.arange(BLOCK)[:, None]
    kj = np.arange(2 * BLOCK)[None, :]
    dist = np.maximum(qi - kj + BLOCK, 0)
    max_exact = REL_BUCKETS // 2
    ratio = np.log(np.maximum(dist, 1).astype(np.float32) / np.float32(max_exact)) / np.float32(
        math.log(REL_MAX_DIST / max_exact))
    large = max_exact + (ratio * np.float32(REL_BUCKETS - max_exact)).astype(np.int32)
    return np.where(dist < max_exact, dist, np.minimum(large, REL_BUCKETS - 1)).astype(np.int32)


def _block_diag(w):
    l, h, d, _ = w.shape
    eye = jnp.eye(h, dtype=w.dtype)
    return (eye[None, :, None, :, None] * w[:, :, :, None, :]).reshape(l, h * d, h * d)


def kernel(x, rel_bias, ln_g, ln_b, ffn_w_gate, ffn_w_up, ffn_w_down, w_in, conv_dw_w, conv_dw_b, conv_ln_g, conv_ln_b, lru_conv_w, lru_conv_b, lru_wa, lru_ba, lru_wx, lru_bx, lru_lambda, attn_sinks, w_out):
    B, S, D = x.shape
    assert D == D_MODEL and S % SEQ_TILE == 0 and (B * S) % FFN_TILE == 0
    rows = lambda v: v.reshape(v.shape[:-1] + (1, v.shape[-1]))
    wg, wu, wd = ffn_w_gate, ffn_w_up, ffn_w_down
    wa, wx = _block_diag(lru_wa).astype(BF16), _block_diag(lru_wx).astype(BF16)
    ln = jnp.stack([ln_g, ln_b], axis=2)
    bucket = _bucket_table()
    for l in range(DEPTH):
        x = _ffn_call(x.reshape(B * S, D), wg, wu, wd, ln, l, 0, norm_input=False).reshape(B, S, D)
        z = _mixer_call(
            x, l, w_in, w_out, bucket, rel_bias,
            conv_dw_w, rows(conv_dw_b), rows(conv_ln_g), rows(conv_ln_b),
            lru_conv_w, rows(lru_conv_b), wa, rows(lru_ba), wx, rows(lru_bx), rows(lru_lambda),
            attn_sinks)
        x = _ffn_call(z.reshape(B * S, D), wg, wu, wd, ln, l, 1, norm_input=True).reshape(B, S, D)
    return x
```
